```python
import jax, jax.numpy as jnp
from jax import lax
import numpy as np

D_MODEL = 1024
BATCH = 8
SEQ = 4096
DEPTH = 2
DEC_BATCH = 1
DEC_SEQ = 16384
PAST_LEN = 128

N_MEM = 256
D_FF = 2816
CONV_DIM = 512
CONV_WIDTH = 31
RET_HEADS = 8
RET_HEAD_DIM = 64
RET_DIM = RET_HEADS * RET_HEAD_DIM
RET_CHUNK = 128
XATTN_HEADS = 4
XATTN_HEAD_DIM = D_MODEL // XATTN_HEADS
ROPE_BASE = 10000.0
NORM_EPS = 1e-6
N_NORMS = 8
IN_COLS = 2 * CONV_DIM + 4 * RET_DIM + 2 * D_MODEL

kernel_name = 'hybrid_conv_retention_encoder'


def rmsnorm(x, g):
    xf = x.astype(jnp.float32)
    y = xf * lax.rsqrt(jnp.mean(xf * xf, axis=-1, keepdims=True) + NORM_EPS)
    return (y * g.astype(jnp.float32)).astype(x.dtype)


def swiglu_ffn(h, w_gu, w_down):
    gate, up = jnp.split(h @ w_gu, 2, axis=-1)
    return (jax.nn.silu(gate) * up) @ w_down


def rotary(x):
    s, dh = x.shape[1], x.shape[-1]
    half = dh // 2
    inv_freq = ROPE_BASE ** (-jnp.arange(half, dtype=jnp.float32) / half)
    ang = jnp.arange(s, dtype=jnp.float32)[:, None] * inv_freq[None, :]
    cos = jnp.cos(ang)[None, :, None, :]
    sin = jnp.sin(ang)[None, :, None, :]
    xf = x.astype(jnp.float32)
    x1, x2 = xf[..., :half], xf[..., half:]
    return jnp.concatenate([x1 * cos - x2 * sin, x1 * sin + x2 * cos], axis=-1)


def retention_direction(q, k, v, decay_param, include_diag):
    b, s, h, dh = q.shape
    c = RET_CHUNK
    n = s // c
    qc = q.reshape(b, n, c, h, dh)
    kc = k.reshape(b, n, c, h, dh)
    vc = v.reshape(b, n, c, h, dh)
    log_gamma = -jnp.exp(decay_param.astype(jnp.float32))
    pos = jnp.arange(c, dtype=jnp.float32)
    rel = pos[:, None] - pos[None, :]
    mask = (rel >= 0) if include_diag else (rel > 0)
    intra_decay = jnp.where(mask[None], jnp.exp(log_gamma[:, None, None] * jnp.where(mask, rel, 0.0)[None]), 0.0)
    scores = jnp.einsum('bnihd,bnjhd->bnhij', qc, kc) * intra_decay[None, None]
    intra = jnp.einsum('bnhij,bnjhd->bnihd', scores, vc)
    k_dec = jnp.exp(log_gamma[None, :] * (c - 1.0 - pos)[:, None])
    chunk_kv = jnp.einsum('bnjhd,bnjhe->nbhde', kc * k_dec[None, None, :, :, None], vc)
    chunk_decay = jnp.exp(log_gamma * c)[None, :, None, None]

    def step(state, kv):
        return state * chunk_decay + kv, state

    _, prev = lax.scan(step, jnp.zeros((b, h, dh, dh), jnp.float32), chunk_kv)
    q_dec = jnp.exp(log_gamma[None, :] * (pos + 1.0)[:, None])
    cross = jnp.einsum('bnihd,nbhde->bnihe', qc * q_dec[None, None, :, :, None], prev)
    return (intra + cross).reshape(b, s, h, dh)


def retention_branch(q, k, v, g, decay_fwd, decay_bwd, gn_g, w_out):
    b, s, _ = q.shape
    shp = (b, s, RET_HEADS, RET_HEAD_DIM)
    qr = rotary(q.reshape(shp))
    kr = rotary(k.reshape(shp)) * (RET_HEAD_DIM ** -0.5)
    vf = v.reshape(shp).astype(jnp.float32)
    fwd = retention_direction(qr, kr, vf, decay_fwd, True)
    bwd = jnp.flip(retention_direction(jnp.flip(qr, 1), jnp.flip(kr, 1), jnp.flip(vf, 1), decay_bwd, False), 1)
    y = fwd + bwd
    mu = jnp.mean(y, axis=-1, keepdims=True)
    var = jnp.mean(jnp.square(y - mu), axis=-1, keepdims=True)
    y = (y - mu) * lax.rsqrt(var + NORM_EPS) * gn_g.astype(jnp.float32).reshape(RET_HEADS, RET_HEAD_DIM)
    y = y.reshape(b, s, RET_DIM).astype(q.dtype)
    return (jax.nn.silu(g) * y) @ w_out


def conformer_conv_branch(a, dw_w, dw_b, ln_g, ln_b, w_pw):
    val, gate = jnp.split(a, 2, axis=-1)
    u = val * jax.nn.sigmoid(gate)
    pad = CONV_WIDTH // 2
    u = lax.conv_general_dilated(u, dw_w[:, None, :].astype(u.dtype), window_strides=(1,), padding=[(pad, pad)],
                                 dimension_numbers=('NWC', 'WIO', 'NWC'), feature_group_count=CONV_DIM) + dw_b
    uf = u.astype(jnp.float32)
    mu = jnp.mean(uf, axis=-1, keepdims=True)
    var = jnp.mean(jnp.square(uf - mu), axis=-1, keepdims=True)
    uf = (uf - mu) * lax.rsqrt(var + NORM_EPS) * ln_g.astype(jnp.float32) + ln_b.astype(jnp.float32)
    return jax.nn.silu(uf).astype(a.dtype) @ w_pw


def memory_cross_attention(h, mem, mem_g, w_q, w_kv, w_o):
    b, s, _ = h.shape
    m = mem.shape[1]
    q = (h @ w_q).reshape(b, s, XATTN_HEADS, XATTN_HEAD_DIM)
    kv = (rmsnorm(mem, mem_g) @ w_kv).reshape(b, m, 2, XATTN_HEADS, XATTN_HEAD_DIM)
    k, v = kv[:, :, 0], kv[:, :, 1]
    logits = jnp.einsum('bshd,bmhd->bhsm', q.astype(jnp.float32), k.astype(jnp.float32)) * (XATTN_HEAD_DIM ** -0.5)
    p = jax.nn.softmax(logits, axis=-1).astype(v.dtype)
    o = jnp.einsum('bhsm,bmhd->bshd', p, v).reshape(b, s, D_MODEL)
    return o @ w_o


def trunk(x, mem, norm_g, ffn1_w_gu, ffn1_w_down, w_in, conv_dw_w, conv_dw_b, conv_ln_g, conv_ln_b, conv_w_pw,
          ret_decay_fwd, ret_decay_bwd, ret_gn_g, ret_w_out, gate_b, w_mix_out, mem_norm_g,
          xattn_w_q, xattn_w_kv, xattn_w_o, ffn2_w_gu, ffn2_w_down):
    splits = [2 * CONV_DIM, 2 * CONV_DIM + RET_DIM, 2 * CONV_DIM + 2 * RET_DIM,
              2 * CONV_DIM + 3 * RET_DIM, 2 * CONV_DIM + 4 * RET_DIM]
    for l in range(DEPTH):
        f = swiglu_ffn(rmsnorm(x, norm_g[l, 0]), ffn1_w_gu[l], ffn1_w_down[l])
        x = x + 0.5 * rmsnorm(f, norm_g[l, 1])
        h = rmsnorm(x, norm_g[l, 2])
        proj = h @ w_in[l]
        conv_in, q, k, v, g, gate_logits = jnp.split(proj, splits, axis=-1)
        conv_out = conformer_conv_branch(conv_in, conv_dw_w[l], conv_dw_b[l], conv_ln_g[l], conv_ln_b[l], conv_w_pw[l])
        ret_out = retention_branch(q, k, v, g, ret_decay_fwd[l], ret_decay_bwd[l], ret_gn_g[l], ret_w_out[l])
        gate_conv, gate_ret = jnp.split(jax.nn.sigmoid(gate_logits + gate_b[l]), 2, axis=-1)
        mixed = (gate_conv * conv_out + gate_ret * ret_out) @ w_mix_out[l]
        x = x + rmsnorm(mixed, norm_g[l, 3])
        a = memory_cross_attention(rmsnorm(x, norm_g[l, 4]), mem, mem_norm_g[l], xattn_w_q[l], xattn_w_kv[l], xattn_w_o[l])
        x = x + rmsnorm(a, norm_g[l, 5])
        f = swiglu_ffn(rmsnorm(x, norm_g[l, 6]), ffn2_w_gu[l], ffn2_w_down[l])
        x = x + 0.5 * rmsnorm(f, norm_g[l, 7])
    return x


def setup_inputs(seed: int = 0) -> dict:
    key = jax.random.key(seed)
    ks = jax.random.split(key, 32)

    def w(k, shape, fan_in):
        return jax.random.normal(k, shape, jnp.float32) * (fan_in ** -0.5)

    def gain(k, shape):
        return 1.0 + 0.05 * jax.random.normal(k, shape, jnp.float32)

    def small(k, shape):
        return 0.02 * jax.random.normal(k, shape, jnp.float32)

    decay_base = jnp.log(-jnp.log1p(-(2.0 ** (-5.0 - jnp.arange(RET_HEADS, dtype=jnp.float32)))))
    return {
        'x_prompt': jax.random.normal(ks[0], (BATCH, SEQ, D_MODEL), jnp.float32),
        'x_sample': jax.random.normal(ks[1], (DEC_BATCH, DEC_SEQ, D_MODEL), jnp.float32),
        'mem_prompt': jax.random.normal(ks[2], (BATCH, N_MEM, D_MODEL), jnp.float32),
        'mem_sample': jax.random.normal(ks[3], (DEC_BATCH, N_MEM, D_MODEL), jnp.float32),
        'norm_g': gain(ks[4], (DEPTH, N_NORMS, D_MODEL)),
        'ffn1_w_gu': w(ks[5], (DEPTH, D_MODEL, 2 * D_FF), D_MODEL),
        'ffn1_w_down': w(ks[6], (DEPTH, D_FF, D_MODEL), D_FF),
        'w_in': w(ks[7], (DEPTH, D_MODEL, IN_COLS), D_MODEL),
        'conv_dw_w': w(ks[8], (DEPTH, CONV_WIDTH, CONV_DIM), CONV_WIDTH),
        'conv_dw_b': small(ks[9], (DEPTH, CONV_DIM)),
        'conv_ln_g': gain(ks[10], (DEPTH, CONV_DIM)),
        'conv_ln_b': small(ks[11], (DEPTH, CONV_DIM)),
        'conv_w_pw': w(ks[12], (DEPTH, CONV_DIM, D_MODEL), CONV_DIM),
        'ret_decay_fwd': decay_base[None, :] + 0.1 * jax.random.normal(ks[13], (DEPTH, RET_HEADS), jnp.float32),
        'ret_decay_bwd': decay_base[None, :] + 0.1 * jax.random.normal(ks[14], (DEPTH, RET_HEADS), jnp.float32),
        'ret_gn_g': gain(ks[15], (DEPTH, RET_DIM)),
        'ret_w_out': w(ks[16], (DEPTH, RET_DIM, D_MODEL), RET_DIM),
        'gate_b': small(ks[17], (DEPTH, 2 * D_MODEL)),
        'w_mix_out': w(ks[18], (DEPTH, D_MODEL, D_MODEL), D_MODEL),
        'mem_norm_g': gain(ks[19], (DEPTH, D_MODEL)),
        'xattn_w_q': w(ks[20], (DEPTH, D_MODEL, D_MODEL), D_MODEL),
        'xattn_w_kv': w(ks[21], (DEPTH, D_MODEL, 2 * D_MODEL), D_MODEL),
        'xattn_w_o': w(ks[22], (DEPTH, D_MODEL, D_MODEL), D_MODEL),
        'ffn2_w_gu': w(ks[23], (DEPTH, D_MODEL, 2 * D_FF), D_MODEL),
        'ffn2_w_down': w(ks[24], (DEPTH, D_FF, D_MODEL), D_FF),
    }


def reference(x_prompt, x_sample, mem_prompt, mem_sample, norm_g, ffn1_w_gu, ffn1_w_down, w_in,
              conv_dw_w, conv_dw_b, conv_ln_g, conv_ln_b, conv_w_pw, ret_decay_fwd, ret_decay_bwd,
              ret_gn_g, ret_w_out, gate_b, w_mix_out, mem_norm_g, xattn_w_q, xattn_w_kv, xattn_w_o,
              ffn2_w_gu, ffn2_w_down):
    y_prompt = trunk(x_prompt, mem_prompt, norm_g, ffn1_w_gu, ffn1_w_down, w_in, conv_dw_w, conv_dw_b,
                     conv_ln_g, conv_ln_b, conv_w_pw, ret_decay_fwd, ret_decay_bwd, ret_gn_g, ret_w_out,
                     gate_b, w_mix_out, mem_norm_g, xattn_w_q, xattn_w_kv, xattn_w_o, ffn2_w_gu, ffn2_w_down)
    y_sample = trunk(x_sample, mem_sample, norm_g, ffn1_w_gu, ffn1_w_down, w_in, conv_dw_w, conv_dw_b,
                     conv_ln_g, conv_ln_b, conv_w_pw, ret_decay_fwd, ret_decay_bwd, ret_gn_g, ret_w_out,
                     gate_b, w_mix_out, mem_norm_g, xattn_w_q, xattn_w_kv, xattn_w_o, ffn2_w_gu, ffn2_w_down)
    return (y_prompt, y_sample)
```

```python
import functools

import jax
import jax.numpy as jnp
from jax import lax
from jax.experimental import pallas as pl
from jax.experimental.pallas import tpu as pltpu

D_MODEL = 1024
D_FF = 2816
CONV_DIM = 512
CONV_WIDTH = 31
CONV_HALO = 16
RET_HEADS = 8
RET_HEAD_DIM = 64
RET_DIM = RET_HEADS * RET_HEAD_DIM
RET_CHUNK = 128
LANES = 128
N_PAIRS = RET_DIM // LANES
XATTN_HEADS = 4
XATTN_HEAD_DIM = D_MODEL // XATTN_HEADS
ROPE_BASE = 10000.0
NORM_EPS = 1e-6
IN_COLS = 2 * CONV_DIM + 4 * RET_DIM + 2 * D_MODEL

TOKEN_TILE = 512
SEQ_TILE = 512
CONV_ROWS = 32
VMEM_LIMIT_BYTES = 56 * 1024 * 1024

F32 = jnp.float32
BF16 = jnp.bfloat16


def _const_spec(shape):
    nd = len(shape)
    return pl.BlockSpec(shape, lambda *_: (0,) * nd, pipeline_mode=pl.Buffered(1))


def _params(*semantics):
    return pltpu.CompilerParams(dimension_semantics=semantics, vmem_limit_bytes=VMEM_LIMIT_BYTES)


def _rms(x, g):
    return x * lax.rsqrt(jnp.mean(x * x, axis=-1, keepdims=True) + NORM_EPS) * g


def _dot(a, b):
    return jnp.dot(a, b, preferred_element_type=F32)


def _dot_nt(a, b):
    return lax.dot_general(a, b, (((1,), (1,)), ((), ())), preferred_element_type=F32)


def _dot_tn(a, b):
    return lax.dot_general(a, b, (((0,), (0,)), ((), ())), preferred_element_type=F32)


def _ffn_kernel(x_ref, gin_ref, gout_ref, wgu_ref, wd_ref, o_ref):
    x = x_ref[...]
    h = _rms(x, gin_ref[...]).astype(BF16)
    gu = _dot(h, wgu_ref[...])
    gate = gu[:, :D_FF]
    up = gu[:, D_FF:]
    act = (gate * jax.nn.sigmoid(gate) * up).astype(BF16)
    f = _dot(act, wd_ref[...])
    o_ref[...] = x + 0.5 * _rms(f, gout_ref[...])


def _ffn(x, g_in, g_out, w_gu, w_down):
    t = x.shape[0]
    tile = pl.BlockSpec((TOKEN_TILE, D_MODEL), lambda i: (i, 0))
    return pl.pallas_call(
        _ffn_kernel,
        grid=(t // TOKEN_TILE,),
        in_specs=[tile, _const_spec((1, D_MODEL)), _const_spec((1, D_MODEL)),
                  _const_spec((D_MODEL, 2 * D_FF)), _const_spec((D_FF, D_MODEL))],
        out_specs=tile,
        out_shape=jax.ShapeDtypeStruct((t, D_MODEL), F32),
        compiler_params=_params("parallel"),
        name="ffn",
    )(x, g_in, g_out, w_gu, w_down)


def _mixer_in_kernel(x_ref, g_ref, w_ref, gb_ref, cos_ref, sin_ref,
                     u_ref, q_ref, k_ref, v_ref, sg_ref, gate_ref):
    h = _rms(x_ref[...], g_ref[...]).astype(BF16)
    proj = _dot(h, w_ref[...])
    o = 2 * CONV_DIM
    u_ref[...] = (proj[:, :CONV_DIM] * jax.nn.sigmoid(proj[:, CONV_DIM:o])).astype(BF16)

    cos = cos_ref[...]
    sin = sin_ref[...]
    lane = lax.broadcasted_iota(jnp.int32, cos.shape, 1)
    first_half = (lane % RET_HEAD_DIM) < (RET_HEAD_DIM // 2)

    def rotary(base, scale, out_ref):
        for c in range(N_PAIRS):
            xc = proj[:, base + c * LANES:base + (c + 1) * LANES]
            partner = jnp.where(first_half, pltpu.roll(xc, LANES - RET_HEAD_DIM // 2, 1),
                                pltpu.roll(xc, RET_HEAD_DIM // 2, 1))
            out_ref[:, c * LANES:(c + 1) * LANES] = ((xc * cos + partner * sin) * scale).astype(BF16)

    rotary(o, 1.0, q_ref)
    rotary(o + RET_DIM, RET_HEAD_DIM ** -0.5, k_ref)
    v_ref[...] = proj[:, o + 2 * RET_DIM:o + 3 * RET_DIM].astype(BF16)
    g = proj[:, o + 3 * RET_DIM:o + 4 * RET_DIM]
    sg_ref[...] = (g * jax.nn.sigmoid(g)).astype(BF16)
    gate_ref[...] = jax.nn.sigmoid(proj[:, o + 4 * RET_DIM:] + gb_ref[...]).astype(BF16)


def _mixer_in(x, g, w_in, gate_b, cos, sin, seq_len):
    t = x.shape[0]
    tiles_per_seq = seq_len // TOKEN_TILE

    def rows(width):
        return pl.BlockSpec((TOKEN_TILE, width), lambda i: (i, 0))

    rope = pl.BlockSpec((TOKEN_TILE, LANES), lambda i: (i % tiles_per_seq, 0))
    out_widths = (CONV_DIM, RET_DIM, RET_DIM, RET_DIM, RET_DIM, 2 * D_MODEL)
    return pl.pallas_call(
        _mixer_in_kernel,
        grid=(t // TOKEN_TILE,),
        in_specs=[rows(D_MODEL), _const_spec((1, D_MODEL)), _const_spec((D_MODEL, IN_COLS)),
                  _const_spec((1, 2 * D_MODEL)), rope, rope],
        out_specs=[rows(w) for w in out_widths],
        out_shape=[jax.ShapeDtypeStruct((t, w), BF16) for w in out_widths],
        compiler_params=_params("parallel"),
        name="mixer_in",
    )(x, g, w_in, gate_b, cos, sin)


def _pair_log_gamma(dec_ref, p):
    return -jnp.exp(dec_ref[p:p + 1, :]), -jnp.exp(dec_ref[N_PAIRS + p:N_PAIRS + p + 1, :])


def _ret_state_kernel(dec_ref, kf_ref, vf_ref, kb_ref, vb_ref, sf_out, sb_out, sf_ref, sb_ref, *, n_chunks):
    @pl.when(pl.program_id(1) == 0)
    def _():
        sf_ref[...] = jnp.zeros_like(sf_ref)
        sb_ref[...] = jnp.zeros_like(sb_ref)

    ri = lax.broadcasted_iota(jnp.int32, (RET_CHUNK, LANES), 0)
    ci = lax.broadcasted_iota(jnp.int32, (RET_CHUNK, LANES), 1)
    same_head = (ri // RET_HEAD_DIM) == (ci // RET_HEAD_DIM)
    row = ri.astype(F32)
    c = float(RET_CHUNK)

    def chunk_kv(k_ref, v_ref, n, p, k_decay):
        rows = slice(n * RET_CHUNK, (n + 1) * RET_CHUNK)
        cols = slice(p * LANES, (p + 1) * LANES)
        kd = (k_ref[0, rows, cols].astype(F32) * k_decay).astype(BF16)
        return jnp.where(same_head, _dot_tn(kd, v_ref[0, rows, cols]), 0.0)

    for p in range(N_PAIRS):
        lgf, lgb = _pair_log_gamma(dec_ref, p)
        kdec_f = jnp.exp(lgf * (c - 1.0 - row))
        kdec_b = jnp.exp(lgb * row)
        step_f = jnp.exp(lgf * c)
        step_b = jnp.exp(lgb * c)
        s = sf_ref[p]
        for n in range(n_chunks):
            sf_out[0, n, p] = s.astype(BF16)
            s = s * step_f + chunk_kv(kf_ref, vf_ref, n, p, kdec_f)
        sf_ref[p] = s
        s = sb_ref[p]
        for n in reversed(range(n_chunks)):
            sb_out[0, n, p] = s.astype(BF16)
            s = s * step_b + chunk_kv(kb_ref, vb_ref, n, p, kdec_b)
        sb_ref[p] = s


def _ret_state(dec, k, v):
    b, s, _ = k.shape
    nt = s // SEQ_TILE
    n_chunks = SEQ_TILE // RET_CHUNK
    fwd = pl.BlockSpec((1, SEQ_TILE, RET_DIM), lambda i, t: (i, t, 0))
    bwd = pl.BlockSpec((1, SEQ_TILE, RET_DIM), lambda i, t: (i, nt - 1 - t, 0))
    state_shape = (1, n_chunks, N_PAIRS, LANES, LANES)
    out = jax.ShapeDtypeStruct((b, s // RET_CHUNK, N_PAIRS, LANES, LANES), BF16)
    return pl.pallas_call(
        functools.partial(_ret_state_kernel, n_chunks=n_chunks),
        grid=(b, nt),
        in_specs=[_const_spec(dec.shape), fwd, fwd, bwd, bwd],
        out_specs=[pl.BlockSpec(state_shape, lambda i, t: (i, t, 0, 0, 0)),
                   pl.BlockSpec(state_shape, lambda i, t: (i, nt - 1 - t, 0, 0, 0))],
        out_shape=[out, out],
        scratch_shapes=[pltpu.VMEM((N_PAIRS, LANES, LANES), F32), pltpu.VMEM((N_PAIRS, LANES, LANES), F32)],
        compiler_params=_params("arbitrary", "arbitrary"),
        name="ret_state",
    )(dec, k, v, k, v)


def _mixer_kernel(dec_ref, x_ref, uprev_ref, u_ref, unext_ref, q_ref, k_ref, v_ref, sg_ref, gate_ref,
                  sf_ref, sb_ref, dww_ref, dwb_ref, lng_ref, lnb_ref, wpw_ref, gng_ref, wro_ref,
                  wmix_ref, g3_ref, pavg_ref, o_ref, upad_ref, cact_ref, y_ref, *, ts):
    t = pl.program_id(1)
    last = pl.num_programs(1) - 1

    upad_ref[0:CONV_HALO, :] = jnp.where(t > 0, uprev_ref[0].astype(F32), 0.0)
    upad_ref[CONV_HALO:CONV_HALO + ts, :] = u_ref[0].astype(F32)
    upad_ref[CONV_HALO + ts:, :] = jnp.where(t < last, unext_ref[0].astype(F32), 0.0)
    first_tap = CONV_HALO - CONV_WIDTH // 2
    for r0 in range(0, ts, CONV_ROWS):
        acc = jnp.zeros((CONV_ROWS, CONV_DIM), F32) + dwb_ref[...]
        for w in range(CONV_WIDTH):
            start = r0 + first_tap + w
            acc = acc + upad_ref[start:start + CONV_ROWS, :] * dww_ref[w:w + 1, :]
        mu = jnp.mean(acc, axis=-1, keepdims=True)
        d = acc - mu
        var = jnp.mean(d * d, axis=-1, keepdims=True)
        z = d * lax.rsqrt(var + NORM_EPS) * lng_ref[...] + lnb_ref[...]
        cact_ref[r0:r0 + CONV_ROWS, :] = (z * jax.nn.sigmoid(z)).astype(BF16)
    conv_out = _dot(cact_ref[...], wpw_ref[...])

    ri = lax.broadcasted_iota(jnp.int32, (RET_CHUNK, RET_CHUNK), 0)
    ci = lax.broadcasted_iota(jnp.int32, (RET_CHUNK, RET_CHUNK), 1)
    ahead = jnp.maximum(ri - ci, 0).astype(F32)
    behind = jnp.maximum(ci - ri, 0).astype(F32)
    causal = ri >= ci
    low_lanes = ci < RET_HEAD_DIM
    row = ri.astype(F32)
    c = float(RET_CHUNK)
    for p in range(N_PAIRS):
        lgf, lgb = _pair_log_gamma(dec_ref, p)
        qdec_f = jnp.exp(lgf * (row + 1.0))
        qdec_b = jnp.exp(lgb * (c - row))
        decay = []
        for sub in range(2):
            h = 2 * p + sub
            lf = -jnp.exp(dec_ref[2 * N_PAIRS + h:2 * N_PAIRS + h + 1, :])
            lb = -jnp.exp(dec_ref[2 * N_PAIRS + RET_HEADS + h:2 * N_PAIRS + RET_HEADS + h + 1, :])
            decay.append(jnp.where(causal, jnp.exp(lf * ahead), jnp.exp(lb * behind)))
        cols = slice(p * LANES, (p + 1) * LANES)
        for n in range(ts // RET_CHUNK):
            rows = slice(n * RET_CHUNK, (n + 1) * RET_CHUNK)
            qp = q_ref[0, rows, cols]
            kp = k_ref[0, rows, cols]
            vp = v_ref[0, rows, cols]
            intra = []
            for sub in range(2):
                in_head = low_lanes if sub == 0 else jnp.logical_not(low_lanes)
                qh = jnp.where(in_head, qp, jnp.zeros_like(qp))
                scores = (_dot_nt(qh, kp) * decay[sub]).astype(BF16)
                intra.append(_dot(scores, vp))
            qf = qp.astype(F32)
            cross = (_dot((qf * qdec_f).astype(BF16), sf_ref[0, n, p])
                     + _dot((qf * qdec_b).astype(BF16), sb_ref[0, n, p]))
            y_ref[rows, cols] = jnp.where(low_lanes, intra[0], intra[1]) + cross

    y = y_ref[...]
    d = y - _dot(y.astype(BF16), pavg_ref[...])
    var = _dot((d * d).astype(BF16), pavg_ref[...])
    yn = d * lax.rsqrt(var + NORM_EPS) * gng_ref[...]
    ret_out = _dot((sg_ref[0].astype(F32) * yn).astype(BF16), wro_ref[...])

    mixed = (gate_ref[0, :, :D_MODEL].astype(F32) * conv_out
             + gate_ref[0, :, D_MODEL:].astype(F32) * ret_out)
    m = _dot(mixed.astype(BF16), wmix_ref[...])
    o_ref[0] = x_ref[0] + _rms(m, g3_ref[...])


def _mixer(dec, x, u, q, k, v, sg, gates, sf, sb, dw_w, dw_b, ln_g, ln_b, w_pw, gn_g, w_ro, w_mix, g3, pavg):
    b, s, _ = x.shape
    ts = SEQ_TILE
    nt = s // ts
    n_chunks = ts // RET_CHUNK
    halo_per_tile = ts // CONV_HALO
    n_halo = s // CONV_HALO

    def seq(width):
        return pl.BlockSpec((1, ts, width), lambda i, t: (i, t, 0))

    halo_prev = pl.BlockSpec((1, CONV_HALO, CONV_DIM),
                             lambda i, t: (i, jnp.maximum(t * halo_per_tile - 1, 0), 0))
    halo_next = pl.BlockSpec((1, CONV_HALO, CONV_DIM),
                             lambda i, t: (i, jnp.minimum((t + 1) * halo_per_tile, n_halo - 1), 0))
    state = pl.BlockSpec((1, n_chunks, N_PAIRS, LANES, LANES), lambda i, t: (i, t, 0, 0, 0))
    return pl.pallas_call(
        functools.partial(_mixer_kernel, ts=ts),
        grid=(b, nt),
        in_specs=[_const_spec(dec.shape), seq(D_MODEL), halo_prev, seq(CONV_DIM), halo_next,
                  seq(RET_DIM), seq(RET_DIM), seq(RET_DIM), seq(RET_DIM), seq(2 * D_MODEL), state, state,
                  _const_spec((CONV_WIDTH, CONV_DIM)), _const_spec((1, CONV_DIM)), _const_spec((1, CONV_DIM)),
                  _const_spec((1, CONV_DIM)), _const_spec((CONV_DIM, D_MODEL)), _const_spec((1, RET_DIM)),
                  _const_spec((RET_DIM, D_MODEL)), _const_spec((D_MODEL, D_MODEL)), _const_spec((1, D_MODEL)),
                  _const_spec((RET_DIM, RET_DIM))],
        out_specs=seq(D_MODEL),
        out_shape=jax.ShapeDtypeStruct((b, s, D_MODEL), F32),
        scratch_shapes=[pltpu.VMEM((ts + 2 * CONV_HALO, CONV_DIM), F32), pltpu.VMEM((ts, CONV_DIM), BF16),
                        pltpu.VMEM((ts, RET_DIM), F32)],
        compiler_params=_params("parallel", "parallel"),
        name="mixer",
    )(dec, x, u, u, u, q, k, v, sg, gates, sf, sb, dw_w, dw_b, ln_g, ln_b, w_pw, gn_g, w_ro, w_mix, g3, pavg)


def _mem_kv_kernel(m_ref, g_ref, w_ref, o_ref):
    o_ref[...] = _dot(_rms(m_ref[...], g_ref[...]).astype(BF16), w_ref[...]).astype(BF16)


def _mem_kv(mem, g, w_kv):
    t = mem.shape[0]
    n_mem = 256
    return pl.pallas_call(
        _mem_kv_kernel,
        grid=(t // n_mem,),
        in_specs=[pl.BlockSpec((n_mem, D_MODEL), lambda i: (i, 0)), _const_spec((1, D_MODEL)),
                  _const_spec((D_MODEL, 2 * D_MODEL))],
        out_specs=pl.BlockSpec((n_mem, 2 * D_MODEL), lambda i: (i, 0)),
        out_shape=jax.ShapeDtypeStruct((t, 2 * D_MODEL), BF16),
        compiler_params=_params("parallel"),
        name="mem_kv",
    )(mem, g, w_kv)


def _xattn_kernel(x_ref, gin_ref, gout_ref, wq_ref, k_ref, v_ref, wo_ref, o_ref, att_ref):
    x = x_ref[0]
    q = _dot(_rms(x, gin_ref[...]).astype(BF16), wq_ref[...])
    for h in range(XATTN_HEADS):
        cols = slice(h * XATTN_HEAD_DIM, (h + 1) * XATTN_HEAD_DIM)
        logits = _dot_nt(q[:, cols].astype(BF16), k_ref[0, :, cols]) * (XATTN_HEAD_DIM ** -0.5)
        e = jnp.exp(logits - jnp.max(logits, axis=-1, keepdims=True))
        p = e / jnp.sum(e, axis=-1, keepdims=True)
        att_ref[:, cols] = _dot(p.astype(BF16), v_ref[0, :, cols]).astype(BF16)
    a = _dot(att_ref[...], wo_ref[...])
    o_ref[0] = x + _rms(a, gout_ref[...])


def _xattn(x, kv, g_in, g_out, w_q, w_o):
    b, s, _ = x.shape
    n_mem = kv.shape[1]
    ts = SEQ_TILE
    seq = pl.BlockSpec((1, ts, D_MODEL), lambda i, t: (i, t, 0))
    return pl.pallas_call(
        _xattn_kernel,
        grid=(b, s // ts),
        in_specs=[seq, _const_spec((1, D_MODEL)), _const_spec((1, D_MODEL)), _const_spec((D_MODEL, D_MODEL)),
                  pl.BlockSpec((1, n_mem, D_MODEL), lambda i, t: (i, 0, 0)),
                  pl.BlockSpec((1, n_mem, D_MODEL), lambda i, t: (i, 0, 1)),
                  _const_spec((D_MODEL, D_MODEL))],
        out_specs=seq,
        out_shape=jax.ShapeDtypeStruct((b, s, D_MODEL), F32),
        scratch_shapes=[pltpu.VMEM((ts, D_MODEL), BF16)],
        compiler_params=_params("parallel", "parallel"),
        name="xattn",
    )(x, g_in, g_out, w_q, kv, kv, w_o)


def _rope_tables(s):
    half = RET_HEAD_DIM // 2
    inv_freq = ROPE_BASE ** (-jnp.arange(half, dtype=F32) / half)
    ang = jnp.arange(s, dtype=F32)[:, None] * inv_freq[None, :]
    cos, sin = jnp.cos(ang), jnp.sin(ang)
    return (jnp.tile(cos, (1, LANES // half)),
            jnp.tile(jnp.concatenate([-sin, sin], axis=1), (1, LANES // RET_HEAD_DIM)))


def _decay_rows(decay_fwd, decay_bwd):
    pair = lambda d: jnp.repeat(d, RET_HEAD_DIM).reshape(N_PAIRS, LANES)
    head = lambda d: jnp.broadcast_to(d[:, None], (RET_HEADS, LANES))
    return jnp.concatenate([pair(decay_fwd), pair(decay_bwd), head(decay_fwd), head(decay_bwd)], axis=0)


def _trunk(x, mem, layers, cos, sin, pavg):
    b, s, _ = x.shape
    t = b * s
    row = lambda a: a.reshape(1, -1)
    flat = lambda a: a.reshape(t, a.shape[-1])
    seq = lambda a: a.reshape(b, s, a.shape[-1])
    mem2 = mem.reshape(b * mem.shape[1], D_MODEL)
    x = flat(x)
    for p in layers:
        ng = p["norm_g"]
        x = _ffn(x, row(ng[0]), row(ng[1]), p["ffn1_w_gu"], p["ffn1_w_down"])
        u, q, k, v, sg, gates = _mixer_in(x, row(ng[2]), p["w_in"], row(p["gate_b"]), cos, sin, s)
        u, q, k, v, sg, gates = map(seq, (u, q, k, v, sg, gates))
        sf, sb = _ret_state(p["dec"], k, v)
        x = _mixer(p["dec"], seq(x), u, q, k, v, sg, gates, sf, sb, p["conv_dw_w"], row(p["conv_dw_b"]),
                   row(p["conv_ln_g"]), row(p["conv_ln_b"]), p["conv_w_pw"], row(p["ret_gn_g"]),
                   p["ret_w_out"], p["w_mix_out"], row(ng[3]), pavg)
        kv = _mem_kv(mem2, row(p["mem_norm_g"]), p["xattn_w_kv"]).reshape(b, mem.shape[1], 2 * D_MODEL)
        x = _xattn(x, kv, row(ng[4]), row(ng[5]), p["xattn_w_q"], p["xattn_w_o"])
        x = _ffn(flat(x), row(ng[6]), row(ng[7]), p["ffn2_w_gu"], p["ffn2_w_down"])
    return seq(x)


def kernel(x_prompt, x_sample, mem_prompt, mem_sample, norm_g, ffn1_w_gu, ffn1_w_down, w_in, conv_dw_w, conv_dw_b,
           conv_ln_g, conv_ln_b, conv_w_pw, ret_decay_fwd, ret_decay_bwd, ret_gn_g, ret_w_out, gate_b, w_mix_out,
           mem_norm_g, xattn_w_q, xattn_w_kv, xattn_w_o, ffn2_w_gu, ffn2_w_down):
    depth = norm_g.shape[0]
    bf = lambda w: w.astype(BF16)
    layers = []
    for l in range(depth):
        layers.append(dict(
            norm_g=norm_g[l], ffn1_w_gu=bf(ffn1_w_gu[l]), ffn1_w_down=bf(ffn1_w_down[l]), w_in=bf(w_in[l]),
            conv_dw_w=conv_dw_w[l], conv_dw_b=conv_dw_b[l], conv_ln_g=conv_ln_g[l], conv_ln_b=conv_ln_b[l],
            conv_w_pw=bf(conv_w_pw[l]), dec=_decay_rows(ret_decay_fwd[l], ret_decay_bwd[l]),
            ret_gn_g=ret_gn_g[l], ret_w_out=bf(ret_w_out[l]), gate_b=gate_b[l], w_mix_out=bf(w_mix_out[l]),
            mem_norm_g=mem_norm_g[l], xattn_w_q=bf(xattn_w_q[l]), xattn_w_kv=bf(xattn_w_kv[l]),
            xattn_w_o=bf(xattn_w_o[l]), ffn2_w_gu=bf(ffn2_w_gu[l]), ffn2_w_down=bf(ffn2_w_down[l])))
    cos, sin = _rope_tables(max(x_prompt.shape[1], x_sample.shape[1]))
    head_of = jnp.arange(RET_DIM) // RET_HEAD_DIM
    pavg = jnp.where(head_of[:, None] == head_of[None, :], 1.0 / RET_HEAD_DIM, 0.0).astype(BF16)
    return (_trunk(x_prompt, mem_prompt, layers, cos, sin, pavg),
            _trunk(x_sample, mem_sample, layers, cos, sin, pavg))
```

```python
import functools

import jax
import jax.numpy as jnp
from jax import lax
from jax.experimental import pallas as pl
from jax.experimental.pallas import tpu as pltpu

D_MODEL = 1024
D_FF = 2816
CONV_DIM = 512
CONV_WIDTH = 31
CONV_HALO = 16
RET_HEADS = 8
RET_HEAD_DIM = 64
RET_DIM = RET_HEADS * RET_HEAD_DIM
RET_CHUNK = 128
LANES = 128
SUBLANES = 8
N_PAIRS = RET_DIM // LANES
XATTN_HEADS = 4
XATTN_HEAD_DIM = D_MODEL // XATTN_HEADS
ROPE_BASE = 10000.0
NORM_EPS = 1e-6
IN_COLS = 2 * CONV_DIM + 4 * RET_DIM + 2 * D_MODEL

TOKEN_TILE = 512
SEQ_TILE = 512
CONV_ROWS = 32
VMEM_LIMIT_BYTES = 56 * 1024 * 1024

F32 = jnp.float32
BF16 = jnp.bfloat16


def _const_spec(shape):
    nd = len(shape)
    return pl.BlockSpec(shape, lambda *_: (0,) * nd, pipeline_mode=pl.Buffered(1))


def _params(*semantics):
    return pltpu.CompilerParams(dimension_semantics=semantics, vmem_limit_bytes=VMEM_LIMIT_BYTES)


def _rms(x, g):
    return x * lax.rsqrt(jnp.mean(x * x, axis=-1, keepdims=True) + NORM_EPS) * g


def _dot(a, b):
    return jnp.dot(a, b, preferred_element_type=F32)


def _dot_nt(a, b):
    return lax.dot_general(a, b, (((1,), (1,)), ((), ())), preferred_element_type=F32)


def _dot_tn(a, b):
    return lax.dot_general(a, b, (((0,), (0,)), ((), ())), preferred_element_type=F32)


def _pair_log_gamma(dec_ref, p):
    return -jnp.exp(dec_ref[p:p + 1, :]), -jnp.exp(dec_ref[N_PAIRS + p:N_PAIRS + p + 1, :])


def _ffn_kernel(x_ref, gin_ref, gout_ref, wgu_ref, wd_ref, o_ref):
    x = x_ref[...]
    h = _rms(x, gin_ref[...]).astype(BF16)
    gu = _dot(h, wgu_ref[...])
    gate = gu[:, :D_FF]
    up = gu[:, D_FF:]
    act = (gate * jax.nn.sigmoid(gate) * up).astype(BF16)
    f = _dot(act, wd_ref[...])
    o_ref[...] = x + 0.5 * _rms(f, gout_ref[...])


def _ffn(x, g_in, g_out, w_gu, w_down):
    t = x.shape[0]
    tile = pl.BlockSpec((TOKEN_TILE, D_MODEL), lambda i: (i, 0))
    return pl.pallas_call(
        _ffn_kernel,
        grid=(t // TOKEN_TILE,),
        in_specs=[tile, _const_spec((1, D_MODEL)), _const_spec((1, D_MODEL)),
                  _const_spec((D_MODEL, 2 * D_FF)), _const_spec((D_FF, D_MODEL))],
        out_specs=tile,
        out_shape=jax.ShapeDtypeStruct((t, D_MODEL), F32),
        compiler_params=_params("parallel"),
        name="ffn",
    )(x, g_in, g_out, w_gu, w_down)


def _mixer_in_kernel(dec_ref, x_ref, g_ref, w_ref, gb_ref, cos_ref, sin_ref,
                     u_ref, q_ref, qf_ref, qb_ref, kf_ref, kb_ref, k_ref, v_ref, sg_ref, gate_ref):
    h = _rms(x_ref[...], g_ref[...]).astype(BF16)
    proj = _dot(h, w_ref[...])
    o = 2 * CONV_DIM
    u_ref[...] = (proj[:, :CONV_DIM] * jax.nn.sigmoid(proj[:, CONV_DIM:o])).astype(BF16)

    cos = cos_ref[...]
    sin = sin_ref[...]
    lane = lax.broadcasted_iota(jnp.int32, cos.shape, 1)
    first_half = (lane % RET_HEAD_DIM) < (RET_HEAD_DIM // 2)
    row = lax.broadcasted_iota(jnp.int32, (RET_CHUNK, LANES), 0).astype(F32)
    c = float(RET_CHUNK)

    def rotary(base, scale, p):
        xc = proj[:, base + p * LANES:base + (p + 1) * LANES]
        partner = jnp.where(first_half, pltpu.roll(xc, LANES - RET_HEAD_DIM // 2, 1),
                            pltpu.roll(xc, RET_HEAD_DIM // 2, 1))
        return (xc * cos + partner * sin) * scale

    def store_decayed(x, decay, out_ref, cols):
        for n in range(x.shape[0] // RET_CHUNK):
            rows = slice(n * RET_CHUNK, (n + 1) * RET_CHUNK)
            out_ref[rows, cols] = (x[rows] * decay).astype(BF16)

    for p in range(N_PAIRS):
        cols = slice(p * LANES, (p + 1) * LANES)
        lgf, lgb = _pair_log_gamma(dec_ref, p)
        q = rotary(o, 1.0, p)
        q_ref[:, cols] = q.astype(BF16)
        store_decayed(q, jnp.exp(lgf * (row + 1.0)), qf_ref, cols)
        store_decayed(q, jnp.exp(lgb * (c - row)), qb_ref, cols)
        k = rotary(o + RET_DIM, RET_HEAD_DIM ** -0.5, p)
        k_ref[:, cols] = k.astype(BF16)
        store_decayed(k, jnp.exp(lgf * (c - 1.0 - row)), kf_ref, cols)
        store_decayed(k, jnp.exp(lgb * row), kb_ref, cols)
    v_ref[...] = proj[:, o + 2 * RET_DIM:o + 3 * RET_DIM].astype(BF16)
    g = proj[:, o + 3 * RET_DIM:o + 4 * RET_DIM]
    sg_ref[...] = (g * jax.nn.sigmoid(g)).astype(BF16)
    gate_ref[...] = jax.nn.sigmoid(proj[:, o + 4 * RET_DIM:] + gb_ref[...]).astype(BF16)


def _mixer_in(dec, x, g, w_in, gate_b, cos, sin, seq_len):
    t = x.shape[0]
    tiles_per_seq = seq_len // TOKEN_TILE

    def rows(width):
        return pl.BlockSpec((TOKEN_TILE, width), lambda i: (i, 0))

    rope = pl.BlockSpec((TOKEN_TILE, LANES), lambda i: (i % tiles_per_seq, 0))
    out_widths = (CONV_DIM,) + (RET_DIM,) * 8 + (2 * D_MODEL,)
    return pl.pallas_call(
        _mixer_in_kernel,
        grid=(t // TOKEN_TILE,),
        in_specs=[_const_spec(dec.shape), rows(D_MODEL), _const_spec((1, D_MODEL)), _const_spec((D_MODEL, IN_COLS)),
                  _const_spec((1, 2 * D_MODEL)), rope, rope],
        out_specs=[rows(w) for w in out_widths],
        out_shape=[jax.ShapeDtypeStruct((t, w), BF16) for w in out_widths],
        compiler_params=_params("parallel"),
        name="mixer_in",
    )(dec, x, g, w_in, gate_b, cos, sin)


def _ret_state_kernel(dec_ref, kf_ref, vf_ref, kb_ref, vb_ref, sf_out, sb_out, sf_ref, sb_ref, *, n_chunks):
    @pl.when(pl.program_id(1) == 0)
    def _():
        sf_ref[...] = jnp.zeros_like(sf_ref)
        sb_ref[...] = jnp.zeros_like(sb_ref)

    ri = lax.broadcasted_iota(jnp.int32, (RET_CHUNK, LANES), 0)
    ci = lax.broadcasted_iota(jnp.int32, (RET_CHUNK, LANES), 1)
    same_head = (ri // RET_HEAD_DIM) == (ci // RET_HEAD_DIM)
    c = float(RET_CHUNK)

    def chunk_kv(k_ref, v_ref, n, p):
        rows = slice(n * RET_CHUNK, (n + 1) * RET_CHUNK)
        cols = slice(p * LANES, (p + 1) * LANES)
        return jnp.where(same_head, _dot_tn(k_ref[0, rows, cols], v_ref[0, rows, cols]), 0.0)

    for p in range(N_PAIRS):
        lgf, lgb = _pair_log_gamma(dec_ref, p)
        step_f = jnp.exp(lgf * c)
        step_b = jnp.exp(lgb * c)
        s = sf_ref[p]
        for n in range(n_chunks):
            sf_out[0, n, p] = s.astype(BF16)
            s = s * step_f + chunk_kv(kf_ref, vf_ref, n, p)
        sf_ref[p] = s
        s = sb_ref[p]
        for n in reversed(range(n_chunks)):
            sb_out[0, n, p] = s.astype(BF16)
            s = s * step_b + chunk_kv(kb_ref, vb_ref, n, p)
        sb_ref[p] = s


def _ret_state(dec, kf, kb, v):
    b, s, _ = v.shape
    nt = s // SEQ_TILE
    n_chunks = SEQ_TILE // RET_CHUNK
    fwd = pl.BlockSpec((1, SEQ_TILE, RET_DIM), lambda i, t: (i, t, 0))
    bwd = pl.BlockSpec((1, SEQ_TILE, RET_DIM), lambda i, t: (i, nt - 1 - t, 0))
    state_shape = (1, n_chunks, N_PAIRS, LANES, LANES)
    out = jax.ShapeDtypeStruct((b, s // RET_CHUNK, N_PAIRS, LANES, LANES), BF16)
    return pl.pallas_call(
        functools.partial(_ret_state_kernel, n_chunks=n_chunks),
        grid=(b, nt),
        in_specs=[_const_spec(dec.shape), fwd, fwd, bwd, bwd],
        out_specs=[pl.BlockSpec(state_shape, lambda i, t: (i, t, 0, 0, 0)),
                   pl.BlockSpec(state_shape, lambda i, t: (i, nt - 1 - t, 0, 0, 0))],
        out_shape=[out, out],
        scratch_shapes=[pltpu.VMEM((N_PAIRS, LANES, LANES), F32), pltpu.VMEM((N_PAIRS, LANES, LANES), F32)],
        compiler_params=_params("arbitrary", "arbitrary"),
        name="ret_state",
    )(dec, kf, v, kb, v)


def _mixer_kernel(dec_ref, x_ref, uprev_ref, u_ref, unext_ref, q_ref, qf_ref, qb_ref, k_ref, v_ref, sg_ref, gate_ref,
                  sf_ref, sb_ref, dww_ref, dwb_ref, lng_ref, lnb_ref, wpw_ref, gng_ref, wro_ref,
                  wmix_ref, g3_ref, pavg_ref, o_ref, upad_ref, ushift_ref, cact_ref, *, ts):
    t = pl.program_id(1)
    last = pl.num_programs(1) - 1
    n_chunks = ts // RET_CHUNK

    upad_ref[0:CONV_HALO, :] = jnp.where(t > 0, uprev_ref[0].astype(F32), 0.0)
    upad_ref[CONV_HALO:CONV_HALO + ts, :] = u_ref[0].astype(F32)
    upad_ref[CONV_HALO + ts:, :] = jnp.where(t < last, unext_ref[0].astype(F32), 0.0)
    first_tap = CONV_HALO - CONV_WIDTH // 2
    reach = -(-(first_tap + CONV_WIDTH - 1) // SUBLANES) * SUBLANES - SUBLANES

    def conv_chunk(n):
        lo = 0 if n == 0 else n * RET_CHUNK + reach
        hi = (n + 1) * RET_CHUNK + reach
        for r in range(1, SUBLANES):
            ushift_ref[r - 1, lo:hi, :] = upad_ref[lo + r:hi + r, :]
        for r0 in range(n * RET_CHUNK, (n + 1) * RET_CHUNK, CONV_ROWS):
            acc = jnp.zeros((CONV_ROWS, CONV_DIM), F32) + dwb_ref[...]
            for w in range(CONV_WIDTH):
                r = (first_tap + w) % SUBLANES
                base = r0 + first_tap + w - r
                taps = (upad_ref[base:base + CONV_ROWS, :] if r == 0
                        else ushift_ref[r - 1, base:base + CONV_ROWS, :])
                acc = acc + taps * dww_ref[w:w + 1, :]
            mu = jnp.mean(acc, axis=-1, keepdims=True)
            d = acc - mu
            var = jnp.mean(d * d, axis=-1, keepdims=True)
            z = d * lax.rsqrt(var + NORM_EPS) * lng_ref[...] + lnb_ref[...]
            cact_ref[r0:r0 + CONV_ROWS, :] = (z * jax.nn.sigmoid(z)).astype(BF16)

    ri = lax.broadcasted_iota(jnp.int32, (RET_CHUNK, RET_CHUNK), 0)
    ci = lax.broadcasted_iota(jnp.int32, (RET_CHUNK, RET_CHUNK), 1)
    ahead = jnp.maximum(ri - ci, 0).astype(F32)
    behind = jnp.maximum(ci - ri, 0).astype(F32)
    causal = ri >= ci
    head_lanes = (ci < RET_HEAD_DIM, ci >= RET_HEAD_DIM)
    decay = []
    for h in range(RET_HEADS):
        lf = -jnp.exp(dec_ref[2 * N_PAIRS + h:2 * N_PAIRS + h + 1, :])
        lb = -jnp.exp(dec_ref[2 * N_PAIRS + RET_HEADS + h:2 * N_PAIRS + RET_HEADS + h + 1, :])
        decay.append(jnp.where(causal, jnp.exp(lf * ahead), jnp.exp(lb * behind)))

    def retention_chunk(n):
        rows = slice(n * RET_CHUNK, (n + 1) * RET_CHUNK)
        out = []
        for p in range(N_PAIRS):
            cols = slice(p * LANES, (p + 1) * LANES)
            qp = q_ref[0, rows, cols]
            vp = v_ref[0, rows, cols]
            zero = jnp.zeros_like(qp)
            scores = _dot_nt(jnp.concatenate([jnp.where(m, qp, zero) for m in head_lanes], axis=0),
                             k_ref[0, rows, cols])
            lhs = [(scores[sub * RET_CHUNK:(sub + 1) * RET_CHUNK] * decay[2 * p + sub]).astype(BF16)
                   for sub in range(2)]
            lhs += [qf_ref[0, rows, cols], qb_ref[0, rows, cols]]
            rhs = [jnp.where(m, vp, zero) for m in head_lanes] + [sf_ref[0, n, p], sb_ref[0, n, p]]
            out.append(_dot(jnp.concatenate(lhs, axis=1), jnp.concatenate(rhs, axis=0)))
        return jnp.concatenate(out, axis=1)

    def project(rows, y):
        conv_out = _dot(cact_ref[rows, :], wpw_ref[...])
        d = y - _dot(y.astype(BF16), pavg_ref[...])
        var = _dot((d * d).astype(BF16), pavg_ref[...])
        yn = d * lax.rsqrt(var + NORM_EPS) * gng_ref[...]
        ret_out = _dot((sg_ref[0, rows, :].astype(F32) * yn).astype(BF16), wro_ref[...])
        mixed = (gate_ref[0, rows, :D_MODEL].astype(F32) * conv_out
                 + gate_ref[0, rows, D_MODEL:].astype(F32) * ret_out)
        m = _dot(mixed.astype(BF16), wmix_ref[...])
        o_ref[0, rows, :] = x_ref[0, rows, :] + _rms(m, g3_ref[...])

    for n in range(n_chunks):
        conv_chunk(n)
    project(slice(0, ts), jnp.concatenate([retention_chunk(n) for n in range(n_chunks)], axis=0))


def _mixer(dec, x, u, q, qf, qb, k, v, sg, gates, sf, sb, dw_w, dw_b, ln_g, ln_b, w_pw, gn_g, w_ro, w_mix, g3,
           pavg):
    b, s, _ = x.shape
    ts = SEQ_TILE
    nt = s // ts
    n_chunks = ts // RET_CHUNK
    halo_per_tile = ts // CONV_HALO
    n_halo = s // CONV_HALO

    def seq(width):
        return pl.BlockSpec((1, ts, width), lambda i, t: (i, t, 0))

    halo_prev = pl.BlockSpec((1, CONV_HALO, CONV_DIM),
                             lambda i, t: (i, jnp.maximum(t * halo_per_tile - 1, 0), 0))
    halo_next = pl.BlockSpec((1, CONV_HALO, CONV_DIM),
                             lambda i, t: (i, jnp.minimum((t + 1) * halo_per_tile, n_halo - 1), 0))
    state = pl.BlockSpec((1, n_chunks, N_PAIRS, LANES, LANES), lambda i, t: (i, t, 0, 0, 0))
    ret = seq(RET_DIM)
    return pl.pallas_call(
        functools.partial(_mixer_kernel, ts=ts),
        grid=(b, nt),
        in_specs=[_const_spec(dec.shape), seq(D_MODEL), halo_prev, seq(CONV_DIM), halo_next,
                  ret, ret, ret, ret, ret, ret, seq(2 * D_MODEL), state, state,
                  _const_spec((CONV_WIDTH, CONV_DIM)), _const_spec((1, CONV_DIM)), _const_spec((1, CONV_DIM)),
                  _const_spec((1, CONV_DIM)), _const_spec((CONV_DIM, D_MODEL)), _const_spec((1, RET_DIM)),
                  _const_spec((RET_DIM, D_MODEL)), _const_spec((D_MODEL, D_MODEL)), _const_spec((1, D_MODEL)),
                  _const_spec((RET_DIM, RET_DIM))],
        out_specs=seq(D_MODEL),
        out_shape=jax.ShapeDtypeStruct((b, s, D_MODEL), F32),
        scratch_shapes=[pltpu.VMEM((ts + 2 * CONV_HALO, CONV_DIM), F32),
                        pltpu.VMEM((SUBLANES - 1, ts + 2 * CONV_HALO - SUBLANES, CONV_DIM), F32),
                        pltpu.VMEM((ts, CONV_DIM), BF16)],
        compiler_params=_params("parallel", "parallel"),
        name="mixer",
    )(dec, x, u, u, u, q, qf, qb, k, v, sg, gates, sf, sb, dw_w, dw_b, ln_g, ln_b, w_pw, gn_g, w_ro, w_mix, g3,
      pavg)


def _mem_kv_kernel(m_ref, g_ref, w_ref, o_ref):
    o_ref[...] = _dot(_rms(m_ref[...], g_ref[...]).astype(BF16), w_ref[...]).astype(BF16)


def _mem_kv(mem, g, w_kv, n_mem):
    t = mem.shape[0]
    return pl.pallas_call(
        _mem_kv_kernel,
        grid=(t // n_mem,),
        in_specs=[pl.BlockSpec((n_mem, D_MODEL), lambda i: (i, 0)), _const_spec((1, D_MODEL)),
                  _const_spec((D_MODEL, 2 * D_MODEL))],
        out_specs=pl.BlockSpec((n_mem, 2 * D_MODEL), lambda i: (i, 0)),
        out_shape=jax.ShapeDtypeStruct((t, 2 * D_MODEL), BF16),
        compiler_params=_params("parallel"),
        name="mem_kv",
    )(mem, g, w_kv)


def _xattn_kernel(x_ref, gin_ref, gout_ref, wq_ref, k_ref, v_ref, wo_ref, o_ref, att_ref):
    x = x_ref[0]
    q = _dot(_rms(x, gin_ref[...]).astype(BF16), wq_ref[...])
    for h in range(XATTN_HEADS):
        cols = slice(h * XATTN_HEAD_DIM, (h + 1) * XATTN_HEAD_DIM)
        logits = _dot_nt(q[:, cols].astype(BF16), k_ref[0, :, cols]) * (XATTN_HEAD_DIM ** -0.5)
        e = jnp.exp(logits - jnp.max(logits, axis=-1, keepdims=True))
        p = e / jnp.sum(e, axis=-1, keepdims=True)
        att_ref[:, cols] = _dot(p.astype(BF16), v_ref[0, :, cols]).astype(BF16)
    a = _dot(att_ref[...], wo_ref[...])
    o_ref[0] = x + _rms(a, gout_ref[...])


def _xattn(x, kv, g_in, g_out, w_q, w_o):
    b, s, _ = x.shape
    n_mem = kv.shape[1]
    ts = SEQ_TILE
    seq = pl.BlockSpec((1, ts, D_MODEL), lambda i, t: (i, t, 0))
    return pl.pallas_call(
        _xattn_kernel,
        grid=(b, s // ts),
        in_specs=[seq, _const_spec((1, D_MODEL)), _const_spec((1, D_MODEL)), _const_spec((D_MODEL, D_MODEL)),
                  pl.BlockSpec((1, n_mem, D_MODEL), lambda i, t: (i, 0, 0)),
                  pl.BlockSpec((1, n_mem, D_MODEL), lambda i, t: (i, 0, 1)),
                  _const_spec((D_MODEL, D_MODEL))],
        out_specs=seq,
        out_shape=jax.ShapeDtypeStruct((b, s, D_MODEL), F32),
        scratch_shapes=[pltpu.VMEM((ts, D_MODEL), BF16)],
        compiler_params=_params("parallel", "parallel"),
        name="xattn",
    )(x, g_in, g_out, w_q, kv, kv, w_o)


def _rope_tables(s):
    half = RET_HEAD_DIM // 2
    inv_freq = ROPE_BASE ** (-jnp.arange(half, dtype=F32) / half)
    ang = jnp.arange(s, dtype=F32)[:, None] * inv_freq[None, :]
    cos, sin = jnp.cos(ang), jnp.sin(ang)
    return (jnp.tile(cos, (1, LANES // half)),
            jnp.tile(jnp.concatenate([-sin, sin], axis=1), (1, LANES // RET_HEAD_DIM)))


def _decay_rows(decay_fwd, decay_bwd):
    pair = lambda d: jnp.repeat(d, RET_HEAD_DIM).reshape(N_PAIRS, LANES)
    head = lambda d: jnp.broadcast_to(d[:, None], (RET_HEADS, LANES))
    return jnp.concatenate([pair(decay_fwd), pair(decay_bwd), head(decay_fwd), head(decay_bwd)], axis=0)


def _trunk(x, mem, layers, cos, sin, pavg):
    b, s, _ = x.shape
    t = b * s
    n_mem = mem.shape[1]
    row = lambda a: a.reshape(1, -1)
    flat = lambda a: a.reshape(t, a.shape[-1])
    seq = lambda a: a.reshape(b, s, a.shape[-1])
    mem2 = mem.reshape(b * n_mem, D_MODEL)
    x = flat(x)
    for p in layers:
        ng = p["norm_g"]
        x = _ffn(x, row(ng[0]), row(ng[1]), p["ffn1_w_gu"], p["ffn1_w_down"])
        proj = _mixer_in(p["dec"], x, row(ng[2]), p["w_in"], row(p["gate_b"]), cos, sin, s)
        u, q, qf, qb, kf, kb, k, v, sg, gates = map(seq, proj)
        sf, sb = _ret_state(p["dec"], kf, kb, v)
        x = _mixer(p["dec"], seq(x), u, q, qf, qb, k, v, sg, gates, sf, sb, p["conv_dw_w"], row(p["conv_dw_b"]),
                   row(p["conv_ln_g"]), row(p["conv_ln_b"]), p["conv_w_pw"], row(p["ret_gn_g"]),
                   p["ret_w_out"], p["w_mix_out"], row(ng[3]), pavg)
        kv = _mem_kv(mem2, row(p["mem_norm_g"]), p["xattn_w_kv"], n_mem).reshape(b, n_mem, 2 * D_MODEL)
        x = _xattn(x, kv, row(ng[4]), row(ng[5]), p["xattn_w_q"], p["xattn_w_o"])
        x = _ffn(flat(x), row(ng[6]), row(ng[7]), p["ffn2_w_gu"], p["ffn2_w_down"])
    return seq(x)


def kernel(x_prompt, x_sample, mem_prompt, mem_sample, norm_g, ffn1_w_gu, ffn1_w_down, w_in, conv_dw_w, conv_dw_b,
           conv_ln_g, conv_ln_b, conv_w_pw, ret_decay_fwd, ret_decay_bwd, ret_gn_g, ret_w_out, gate_b, w_mix_out,
           mem_norm_g, xattn_w_q, xattn_w_kv, xattn_w_o, ffn2_w_gu, ffn2_w_down):
    depth = norm_g.shape[0]
    bf = lambda w: w.astype(BF16)
    layers = []
    for l in range(depth):
        layers.append(dict(
            norm_g=norm_g[l], ffn1_w_gu=bf(ffn1_w_gu[l]), ffn1_w_down=bf(ffn1_w_down[l]), w_in=bf(w_in[l]),
            conv_dw_w=conv_dw_w[l], conv_dw_b=conv_dw_b[l], conv_ln_g=conv_ln_g[l], conv_ln_b=conv_ln_b[l],
            conv_w_pw=bf(conv_w_pw[l]), dec=_decay_rows(ret_decay_fwd[l], ret_decay_bwd[l]),
            ret_gn_g=ret_gn_g[l], ret_w_out=bf(ret_w_out[l]), gate_b=gate_b[l], w_mix_out=bf(w_mix_out[l]),
            mem_norm_g=mem_norm_g[l], xattn_w_q=bf(xattn_w_q[l]), xattn_w_kv=bf(xattn_w_kv[l]),
            xattn_w_o=bf(xattn_w_o[l]), ffn2_w_gu=bf(ffn2_w_gu[l]), ffn2_w_down=bf(ffn2_w_down[l])))
    cos, sin = _rope_tables(max(x_prompt.shape[1], x_sample.shape[1]))
    head_of = jnp.arange(RET_DIM) // RET_HEAD_DIM
    pavg = jnp.where(head_of[:, None] == head_of[None, :], 1.0 / RET_HEAD_DIM, 0.0).astype(BF16)
    return (_trunk(x_prompt, mem_prompt, layers, cos, sin, pavg),
            _trunk(x_sample, mem_sample, layers, cos, sin, pavg))
```

```python
import functools

import jax
import jax.numpy as jnp
from jax import lax
from jax.experimental import pallas as pl
from jax.experimental.pallas import tpu as pltpu

D_MODEL = 1024
D_FF = 2816
CONV_DIM = 512
CONV_WIDTH = 31
CONV_HALO = 16
RET_HEADS = 8
RET_HEAD_DIM = 64
RET_DIM = RET_HEADS * RET_HEAD_DIM
RET_CHUNK = 128
LANES = 128
SUBLANES = 8
N_PAIRS = RET_DIM // LANES
XATTN_HEADS = 4
XATTN_HEAD_DIM = D_MODEL // XATTN_HEADS
ROPE_BASE = 10000.0
NORM_EPS = 1e-6
IN_COLS = 2 * CONV_DIM + 4 * RET_DIM + 2 * D_MODEL

TOKEN_TILE = 512
SEQ_TILE = 512
CONV_ROWS = 32
VMEM_LIMIT_BYTES = 56 * 1024 * 1024

F32 = jnp.float32
BF16 = jnp.bfloat16


def _const_spec(shape):
    nd = len(shape)
    return pl.BlockSpec(shape, lambda *_: (0,) * nd, pipeline_mode=pl.Buffered(1))


def _params(*semantics):
    return pltpu.CompilerParams(dimension_semantics=semantics, vmem_limit_bytes=VMEM_LIMIT_BYTES)


def _rms(x, g):
    return x * lax.rsqrt(jnp.mean(x * x, axis=-1, keepdims=True) + NORM_EPS) * g


def _dot(a, b):
    return jnp.dot(a, b, preferred_element_type=F32)


def _dot_nt(a, b):
    return lax.dot_general(a, b, (((1,), (1,)), ((), ())), preferred_element_type=F32)


def _dot_tn(a, b):
    return lax.dot_general(a, b, (((0,), (0,)), ((), ())), preferred_element_type=F32)


def _pair_log_gamma(dec_ref, p):
    return -jnp.exp(dec_ref[p:p + 1, :]), -jnp.exp(dec_ref[N_PAIRS + p:N_PAIRS + p + 1, :])


def _ffn_kernel(x_ref, gin_ref, gout_ref, wgu_ref, wd_ref, o_ref):
    x = x_ref[...]
    h = _rms(x, gin_ref[...]).astype(BF16)
    gu = _dot(h, wgu_ref[...])
    gate = gu[:, :D_FF]
    up = gu[:, D_FF:]
    act = (gate * jax.nn.sigmoid(gate) * up).astype(BF16)
    f = _dot(act, wd_ref[...])
    o_ref[...] = x + 0.5 * _rms(f, gout_ref[...])


def _ffn(x, g_in, g_out, w_gu, w_down):
    t = x.shape[0]
    tile = pl.BlockSpec((TOKEN_TILE, D_MODEL), lambda i: (i, 0))
    return pl.pallas_call(
        _ffn_kernel,
        grid=(t // TOKEN_TILE,),
        in_specs=[tile, _const_spec((1, D_MODEL)), _const_spec((1, D_MODEL)),
                  _const_spec((D_MODEL, 2 * D_FF)), _const_spec((D_FF, D_MODEL))],
        out_specs=tile,
        out_shape=jax.ShapeDtypeStruct((t, D_MODEL), F32),
        compiler_params=_params("parallel"),
        name="ffn",
    )(x, g_in, g_out, w_gu, w_down)


def _mixer_in_kernel(dec_ref, x_ref, g_ref, w_ref, gb_ref, cos_ref, sin_ref,
                     u_ref, q_ref, qf_ref, qb_ref, kf_ref, kb_ref, k_ref, v_ref, sg_ref, gate_ref):
    h = _rms(x_ref[...], g_ref[...]).astype(BF16)
    proj = _dot(h, w_ref[...])
    o = 2 * CONV_DIM
    u_ref[...] = (proj[:, :CONV_DIM] * jax.nn.sigmoid(proj[:, CONV_DIM:o])).astype(BF16)

    cos = cos_ref[...]
    sin = sin_ref[...]
    lane = lax.broadcasted_iota(jnp.int32, cos.shape, 1)
    first_half = (lane % RET_HEAD_DIM) < (RET_HEAD_DIM // 2)
    row = lax.broadcasted_iota(jnp.int32, (RET_CHUNK, LANES), 0).astype(F32)
    c = float(RET_CHUNK)

    def rotary(base, scale, p):
        xc = proj[:, base + p * LANES:base + (p + 1) * LANES]
        partner = jnp.where(first_half, pltpu.roll(xc, LANES - RET_HEAD_DIM // 2, 1),
                            pltpu.roll(xc, RET_HEAD_DIM // 2, 1))
        return (xc * cos + partner * sin) * scale

    def store_decayed(x, decay, out_ref, cols):
        for n in range(x.shape[0] // RET_CHUNK):
            rows = slice(n * RET_CHUNK, (n + 1) * RET_CHUNK)
            out_ref[rows, cols] = (x[rows] * decay).astype(BF16)

    for p in range(N_PAIRS):
        cols = slice(p * LANES, (p + 1) * LANES)
        lgf, lgb = _pair_log_gamma(dec_ref, p)
        q = rotary(o, 1.0, p)
        q_ref[:, cols] = q.astype(BF16)
        store_decayed(q, jnp.exp(lgf * (row + 1.0)), qf_ref, cols)
        store_decayed(q, jnp.exp(lgb * (c - row)), qb_ref, cols)
        k = rotary(o + RET_DIM, RET_HEAD_DIM ** -0.5, p)
        k_ref[:, cols] = k.astype(BF16)
        store_decayed(k, jnp.exp(lgf * (c - 1.0 - row)), kf_ref, cols)
        store_decayed(k, jnp.exp(lgb * row), kb_ref, cols)
    v_ref[...] = proj[:, o + 2 * RET_DIM:o + 3 * RET_DIM].astype(BF16)
    g = proj[:, o + 3 * RET_DIM:o + 4 * RET_DIM]
    sg_ref[...] = (g * jax.nn.sigmoid(g)).astype(BF16)
    gate_ref[...] = jax.nn.sigmoid(proj[:, o + 4 * RET_DIM:] + gb_ref[...]).astype(BF16)


def _mixer_in(dec, x, g, w_in, gate_b, cos, sin, seq_len):
    t = x.shape[0]
    tiles_per_seq = seq_len // TOKEN_TILE

    def rows(width):
        return pl.BlockSpec((TOKEN_TILE, width), lambda i: (i, 0))

    rope = pl.BlockSpec((TOKEN_TILE, LANES), lambda i: (i % tiles_per_seq, 0))
    out_widths = (CONV_DIM,) + (RET_DIM,) * 8 + (2 * D_MODEL,)
    return pl.pallas_call(
        _mixer_in_kernel,
        grid=(t // TOKEN_TILE,),
        in_specs=[_const_spec(dec.shape), rows(D_MODEL), _const_spec((1, D_MODEL)), _const_spec((D_MODEL, IN_COLS)),
                  _const_spec((1, 2 * D_MODEL)), rope, rope],
        out_specs=[rows(w) for w in out_widths],
        out_shape=[jax.ShapeDtypeStruct((t, w), BF16) for w in out_widths],
        compiler_params=_params("parallel"),
        name="mixer_in",
    )(dec, x, g, w_in, gate_b, cos, sin)


def _ret_state_kernel(dec_ref, kf_ref, vf_ref, kb_ref, vb_ref, sf_out, sb_out, sf_ref, sb_ref, *, n_chunks):
    @pl.when(pl.program_id(1) == 0)
    def _():
        sf_ref[...] = jnp.zeros_like(sf_ref)
        sb_ref[...] = jnp.zeros_like(sb_ref)

    ri = lax.broadcasted_iota(jnp.int32, (RET_CHUNK, LANES), 0)
    ci = lax.broadcasted_iota(jnp.int32, (RET_CHUNK, LANES), 1)
    same_head = (ri // RET_HEAD_DIM) == (ci // RET_HEAD_DIM)
    c = float(RET_CHUNK)

    def chunk_kv(k_ref, v_ref, n, p):
        rows = slice(n * RET_CHUNK, (n + 1) * RET_CHUNK)
        cols = slice(p * LANES, (p + 1) * LANES)
        return jnp.where(same_head, _dot_tn(k_ref[0, rows, cols], v_ref[0, rows, cols]), 0.0)

    for p in range(N_PAIRS):
        lgf, lgb = _pair_log_gamma(dec_ref, p)
        step_f = jnp.exp(lgf * c)
        step_b = jnp.exp(lgb * c)
        s = sf_ref[p]
        for n in range(n_chunks):
            sf_out[0, n, p] = s.astype(BF16)
            s = s * step_f + chunk_kv(kf_ref, vf_ref, n, p)
        sf_ref[p] = s
        s = sb_ref[p]
        for n in reversed(range(n_chunks)):
            sb_out[0, n, p] = s.astype(BF16)
            s = s * step_b + chunk_kv(kb_ref, vb_ref, n, p)
        sb_ref[p] = s


def _ret_state(dec, kf, kb, v):
    b, s, _ = v.shape
    nt = s // SEQ_TILE
    n_chunks = SEQ_TILE // RET_CHUNK
    fwd = pl.BlockSpec((1, SEQ_TILE, RET_DIM), lambda i, t: (i, t, 0))
    bwd = pl.BlockSpec((1, SEQ_TILE, RET_DIM), lambda i, t: (i, nt - 1 - t, 0))
    state_shape = (1, n_chunks, N_PAIRS, LANES, LANES)
    out = jax.ShapeDtypeStruct((b, s // RET_CHUNK, N_PAIRS, LANES, LANES), BF16)
    return pl.pallas_call(
        functools.partial(_ret_state_kernel, n_chunks=n_chunks),
        grid=(b, nt),
        in_specs=[_const_spec(dec.shape), fwd, fwd, bwd, bwd],
        out_specs=[pl.BlockSpec(state_shape, lambda i, t: (i, t, 0, 0, 0)),
                   pl.BlockSpec(state_shape, lambda i, t: (i, nt - 1 - t, 0, 0, 0))],
        out_shape=[out, out],
        scratch_shapes=[pltpu.VMEM((N_PAIRS, LANES, LANES), F32), pltpu.VMEM((N_PAIRS, LANES, LANES), F32)],
        compiler_params=_params("arbitrary", "arbitrary"),
        name="ret_state",
    )(dec, kf, v, kb, v)


def _mixer_kernel(dec_ref, x_ref, uprev_ref, u_ref, unext_ref, q_ref, qf_ref, qb_ref, k_ref, v_ref, sg_ref, gate_ref,
                  sf_ref, sb_ref, dww_ref, dwb_ref, lng_ref, lnb_ref, wpw_ref, gng_ref, wro_ref,
                  wmix_ref, g3_ref, pavg_ref, o_ref, upad_ref, ushift_ref, cact_ref, *, ts):
    t = pl.program_id(1)
    last = pl.num_programs(1) - 1
    n_chunks = ts // RET_CHUNK

    upad_ref[0:CONV_HALO, :] = jnp.where(t > 0, uprev_ref[0].astype(F32), 0.0)
    upad_ref[CONV_HALO:CONV_HALO + ts, :] = u_ref[0].astype(F32)
    upad_ref[CONV_HALO + ts:, :] = jnp.where(t < last, unext_ref[0].astype(F32), 0.0)
    first_tap = CONV_HALO - CONV_WIDTH // 2
    n_shift = ts + 2 * CONV_HALO - SUBLANES
    for r in range(1, SUBLANES):
        ushift_ref[r - 1] = upad_ref[r:r + n_shift, :]
    row_groups = CONV_ROWS // SUBLANES

    def conv_rows(r0, after):
        acc = jnp.concatenate([dwb_ref[...]] * row_groups, axis=0)
        if after is not None:
            bits = pltpu.bitcast(after[0:SUBLANES, 0:LANES], jnp.uint32)
            zero = pltpu.bitcast(lax.shift_right_logical(lax.shift_right_logical(bits, jnp.uint32(31)),
                                                         jnp.uint32(1)), F32)
            acc = acc + jnp.concatenate([jnp.concatenate([zero] * (CONV_DIM // LANES), axis=1)] * row_groups,
                                        axis=0)
        for w in range(CONV_WIDTH):
            r = (first_tap + w) % SUBLANES
            base = r0 + first_tap + w - r
            taps = (upad_ref[base:base + CONV_ROWS, :] if r == 0
                    else ushift_ref[r - 1, base:base + CONV_ROWS, :])
            acc = acc + taps * jnp.concatenate([dww_ref[w]] * row_groups, axis=0)
        mu = jnp.mean(acc, axis=-1, keepdims=True)
        d = acc - mu
        var = jnp.mean(d * d, axis=-1, keepdims=True)
        z = (d * lax.rsqrt(var + NORM_EPS) * jnp.concatenate([lng_ref[...]] * row_groups, axis=0)
             + jnp.concatenate([lnb_ref[...]] * row_groups, axis=0))
        cact_ref[r0:r0 + CONV_ROWS, :] = (z * jax.nn.sigmoid(z)).astype(BF16)

    ri = lax.broadcasted_iota(jnp.int32, (RET_CHUNK, RET_CHUNK), 0)
    ci = lax.broadcasted_iota(jnp.int32, (RET_CHUNK, RET_CHUNK), 1)
    ahead = jnp.maximum(ri - ci, 0).astype(F32)
    behind = jnp.maximum(ci - ri, 0).astype(F32)
    causal = ri >= ci
    head_lanes = (ci < RET_HEAD_DIM, ci >= RET_HEAD_DIM)
    decay = []
    for h in range(RET_HEADS):
        lf = -jnp.exp(dec_ref[2 * N_PAIRS + h:2 * N_PAIRS + h + 1, :])
        lb = -jnp.exp(dec_ref[2 * N_PAIRS + RET_HEADS + h:2 * N_PAIRS + RET_HEADS + h + 1, :])
        decay.append(jnp.where(causal, jnp.exp(lf * ahead), jnp.exp(lb * behind)))

    def retention_unit(n, p):
        rows = slice(n * RET_CHUNK, (n + 1) * RET_CHUNK)
        cols = slice(p * LANES, (p + 1) * LANES)
        qp = q_ref[0, rows, cols]
        vp = v_ref[0, rows, cols]
        zero = jnp.zeros_like(qp)
        scores = _dot_nt(jnp.concatenate([jnp.where(m, qp, zero) for m in head_lanes], axis=0),
                         k_ref[0, rows, cols])
        lhs = [(scores[sub * RET_CHUNK:(sub + 1) * RET_CHUNK] * decay[2 * p + sub]).astype(BF16)
               for sub in range(2)]
        lhs += [qf_ref[0, rows, cols], qb_ref[0, rows, cols]]
        rhs = [jnp.where(m, vp, zero) for m in head_lanes] + [sf_ref[0, n, p], sb_ref[0, n, p]]
        return _dot(jnp.concatenate(lhs, axis=1), jnp.concatenate(rhs, axis=0))

    def project(rows, y):
        conv_out = _dot(cact_ref[rows, :], wpw_ref[...])
        d = y - _dot(y.astype(BF16), pavg_ref[...])
        var = _dot((d * d).astype(BF16), pavg_ref[...])
        yn = d * lax.rsqrt(var + NORM_EPS) * gng_ref[...]
        ret_out = _dot((sg_ref[0, rows, :].astype(F32) * yn).astype(BF16), wro_ref[...])
        mixed = (gate_ref[0, rows, :D_MODEL].astype(F32) * conv_out
                 + gate_ref[0, rows, D_MODEL:].astype(F32) * ret_out)
        m = _dot(mixed.astype(BF16), wmix_ref[...])
        o_ref[0, rows, :] = x_ref[0, rows, :] + _rms(m, g3_ref[...])

    units = [(n, p) for n in range(n_chunks) for p in range(N_PAIRS)]
    conv_blocks = list(range(0, ts, CONV_ROWS))
    per_unit = -(-len(conv_blocks) // len(units))
    y = {}
    for i, (n, p) in enumerate(units):
        y[n, p] = retention_unit(n, p)
        for r0 in conv_blocks[i * per_unit:(i + 1) * per_unit]:
            conv_rows(r0, y[units[i - 1]] if i >= 1 else None)
    y = jnp.concatenate([jnp.concatenate([y[n, p] for p in range(N_PAIRS)], axis=1) for n in range(n_chunks)],
                        axis=0)
    project(slice(0, ts), y)


def _mixer(dec, x, u, q, qf, qb, k, v, sg, gates, sf, sb, dw_w, dw_b, ln_g, ln_b, w_pw, gn_g, w_ro, w_mix, g3,
           pavg):
    b, s, _ = x.shape
    ts = SEQ_TILE
    nt = s // ts
    n_chunks = ts // RET_CHUNK
    halo_per_tile = ts // CONV_HALO
    n_halo = s // CONV_HALO

    def seq(width):
        return pl.BlockSpec((1, ts, width), lambda i, t: (i, t, 0))

    halo_prev = pl.BlockSpec((1, CONV_HALO, CONV_DIM),
                             lambda i, t: (i, jnp.maximum(t * halo_per_tile - 1, 0), 0))
    halo_next = pl.BlockSpec((1, CONV_HALO, CONV_DIM),
                             lambda i, t: (i, jnp.minimum((t + 1) * halo_per_tile, n_halo - 1), 0))
    state = pl.BlockSpec((1, n_chunks, N_PAIRS, LANES, LANES), lambda i, t: (i, t, 0, 0, 0))
    ret = seq(RET_DIM)
    return pl.pallas_call(
        functools.partial(_mixer_kernel, ts=ts),
        grid=(b, nt),
        in_specs=[_const_spec(dec.shape), seq(D_MODEL), halo_prev, seq(CONV_DIM), halo_next,
                  ret, ret, ret, ret, ret, ret, seq(2 * D_MODEL), state, state,
                  _const_spec((CONV_WIDTH, SUBLANES, CONV_DIM)), _const_spec((SUBLANES, CONV_DIM)),
                  _const_spec((SUBLANES, CONV_DIM)), _const_spec((SUBLANES, CONV_DIM)), _const_spec((CONV_DIM, D_MODEL)), _const_spec((1, RET_DIM)),
                  _const_spec((RET_DIM, D_MODEL)), _const_spec((D_MODEL, D_MODEL)), _const_spec((1, D_MODEL)),
                  _const_spec((RET_DIM, RET_DIM))],
        out_specs=seq(D_MODEL),
        out_shape=jax.ShapeDtypeStruct((b, s, D_MODEL), F32),
        scratch_shapes=[pltpu.VMEM((ts + 2 * CONV_HALO, CONV_DIM), F32),
                        pltpu.VMEM((SUBLANES - 1, ts + 2 * CONV_HALO - SUBLANES, CONV_DIM), F32),
                        pltpu.VMEM((ts, CONV_DIM), BF16)],
        compiler_params=_params("parallel", "parallel"),
        name="mixer",
    )(dec, x, u, u, u, q, qf, qb, k, v, sg, gates, sf, sb, dw_w, dw_b, ln_g, ln_b, w_pw, gn_g, w_ro, w_mix, g3,
      pavg)


def _mem_kv_kernel(m_ref, g_ref, w_ref, o_ref):
    o_ref[...] = _dot(_rms(m_ref[...], g_ref[...]).astype(BF16), w_ref[...]).astype(BF16)


def _mem_kv(mem, g, w_kv, n_mem):
    t = mem.shape[0]
    return pl.pallas_call(
        _mem_kv_kernel,
        grid=(t // n_mem,),
        in_specs=[pl.BlockSpec((n_mem, D_MODEL), lambda i: (i, 0)), _const_spec((1, D_MODEL)),
                  _const_spec((D_MODEL, 2 * D_MODEL))],
        out_specs=pl.BlockSpec((n_mem, 2 * D_MODEL), lambda i: (i, 0)),
        out_shape=jax.ShapeDtypeStruct((t, 2 * D_MODEL), BF16),
        compiler_params=_params("parallel"),
        name="mem_kv",
    )(mem, g, w_kv)


def _xattn_kernel(x_ref, gin_ref, gout_ref, wq_ref, k_ref, v_ref, wo_ref, o_ref, att_ref):
    x = x_ref[0]
    q = _dot(_rms(x, gin_ref[...]).astype(BF16), wq_ref[...])
    for h in range(XATTN_HEADS):
        cols = slice(h * XATTN_HEAD_DIM, (h + 1) * XATTN_HEAD_DIM)
        logits = _dot_nt(q[:, cols].astype(BF16), k_ref[0, :, cols]) * (XATTN_HEAD_DIM ** -0.5)
        e = jnp.exp(logits - jnp.max(logits, axis=-1, keepdims=True))
        p = e / jnp.sum(e, axis=-1, keepdims=True)
        att_ref[:, cols] = _dot(p.astype(BF16), v_ref[0, :, cols]).astype(BF16)
    a = _dot(att_ref[...], wo_ref[...])
    o_ref[0] = x + _rms(a, gout_ref[...])


def _xattn(x, kv, g_in, g_out, w_q, w_o):
    b, s, _ = x.shape
    n_mem = kv.shape[1]
    ts = SEQ_TILE
    seq = pl.BlockSpec((1, ts, D_MODEL), lambda i, t: (i, t, 0))
    return pl.pallas_call(
        _xattn_kernel,
        grid=(b, s // ts),
        in_specs=[seq, _const_spec((1, D_MODEL)), _const_spec((1, D_MODEL)), _const_spec((D_MODEL, D_MODEL)),
                  pl.BlockSpec((1, n_mem, D_MODEL), lambda i, t: (i, 0, 0)),
                  pl.BlockSpec((1, n_mem, D_MODEL), lambda i, t: (i, 0, 1)),
                  _const_spec((D_MODEL, D_MODEL))],
        out_specs=seq,
        out_shape=jax.ShapeDtypeStruct((b, s, D_MODEL), F32),
        scratch_shapes=[pltpu.VMEM((ts, D_MODEL), BF16)],
        compiler_params=_params("parallel", "parallel"),
        name="xattn",
    )(x, g_in, g_out, w_q, kv, kv, w_o)


def _rope_tables(s):
    half = RET_HEAD_DIM // 2
    inv_freq = ROPE_BASE ** (-jnp.arange(half, dtype=F32) / half)
    ang = jnp.arange(s, dtype=F32)[:, None] * inv_freq[None, :]
    cos, sin = jnp.cos(ang), jnp.sin(ang)
    return (jnp.tile(cos, (1, LANES // half)),
            jnp.tile(jnp.concatenate([-sin, sin], axis=1), (1, LANES // RET_HEAD_DIM)))


def _decay_rows(decay_fwd, decay_bwd):
    pair = lambda d: jnp.repeat(d, RET_HEAD_DIM).reshape(N_PAIRS, LANES)
    head = lambda d: jnp.broadcast_to(d[:, None], (RET_HEADS, LANES))
    return jnp.concatenate([pair(decay_fwd), pair(decay_bwd), head(decay_fwd), head(decay_bwd)], axis=0)


def _trunk(x, mem, layers, cos, sin, pavg):
    b, s, _ = x.shape
    t = b * s
    n_mem = mem.shape[1]
    row = lambda a: a.reshape(1, -1)
    rows8 = lambda a: jnp.broadcast_to(a[..., None, :], a.shape[:-1] + (SUBLANES, a.shape[-1]))
    flat = lambda a: a.reshape(t, a.shape[-1])
    seq = lambda a: a.reshape(b, s, a.shape[-1])
    mem2 = mem.reshape(b * n_mem, D_MODEL)
    x = flat(x)
    for p in layers:
        ng = p["norm_g"]
        x = _ffn(x, row(ng[0]), row(ng[1]), p["ffn1_w_gu"], p["ffn1_w_down"])
        proj = _mixer_in(p["dec"], x, row(ng[2]), p["w_in"], row(p["gate_b"]), cos, sin, s)
        u, q, qf, qb, kf, kb, k, v, sg, gates = map(seq, proj)
        sf, sb = _ret_state(p["dec"], kf, kb, v)
        x = _mixer(p["dec"], seq(x), u, q, qf, qb, k, v, sg, gates, sf, sb, rows8(p["conv_dw_w"]),
                   rows8(p["conv_dw_b"]), rows8(p["conv_ln_g"]), rows8(p["conv_ln_b"]), p["conv_w_pw"],
                   row(p["ret_gn_g"]),
                   p["ret_w_out"], p["w_mix_out"], row(ng[3]), pavg)
        kv = _mem_kv(mem2, row(p["mem_norm_g"]), p["xattn_w_kv"], n_mem).reshape(b, n_mem, 2 * D_MODEL)
        x = _xattn(x, kv, row(ng[4]), row(ng[5]), p["xattn_w_q"], p["xattn_w_o"])
        x = _ffn(flat(x), row(ng[6]), row(ng[7]), p["ffn2_w_gu"], p["ffn2_w_down"])
    return seq(x)


def kernel(x_prompt, x_sample, mem_prompt, mem_sample, norm_g, ffn1_w_gu, ffn1_w_down, w_in, conv_dw_w, conv_dw_b,
           conv_ln_g, conv_ln_b, conv_w_pw, ret_decay_fwd, ret_decay_bwd, ret_gn_g, ret_w_out, gate_b, w_mix_out,
           mem_norm_g, xattn_w_q, xattn_w_kv, xattn_w_o, ffn2_w_gu, ffn2_w_down):
    depth = norm_g.shape[0]
    bf = lambda w: w.astype(BF16)
    layers = []
    for l in range(depth):
        layers.append(dict(
            norm_g=norm_g[l], ffn1_w_gu=bf(ffn1_w_gu[l]), ffn1_w_down=bf(ffn1_w_down[l]), w_in=bf(w_in[l]),
            conv_dw_w=conv_dw_w[l], conv_dw_b=conv_dw_b[l], conv_ln_g=conv_ln_g[l], conv_ln_b=conv_ln_b[l],
            conv_w_pw=bf(conv_w_pw[l]), dec=_decay_rows(ret_decay_fwd[l], ret_decay_bwd[l]),
            ret_gn_g=ret_gn_g[l], ret_w_out=bf(ret_w_out[l]), gate_b=gate_b[l], w_mix_out=bf(w_mix_out[l]),
            mem_norm_g=mem_norm_g[l], xattn_w_q=bf(xattn_w_q[l]), xattn_w_kv=bf(xattn_w_kv[l]),
            xattn_w_o=bf(xattn_w_o[l]), ffn2_w_gu=bf(ffn2_w_gu[l]), ffn2_w_down=bf(ffn2_w_down[l])))
    cos, sin = _rope_tables(max(x_prompt.shape[1], x_sample.shape[1]))
    head_of = jnp.arange(RET_DIM) // RET_HEAD_DIM
    pavg = jnp.where(head_of[:, None] == head_of[None, :], 1.0 / RET_HEAD_DIM, 0.0).astype(BF16)
    return (_trunk(x_prompt, mem_prompt, layers, cos, sin, pavg),
            _trunk(x_sample, mem_sample, layers, cos, sin, pavg))
```

```python
import functools

import jax
import jax.numpy as jnp
from jax import lax
from jax.experimental import pallas as pl
from jax.experimental.pallas import tpu as pltpu

D_MODEL = 1024
D_FF = 2816
CONV_DIM = 512
CONV_WIDTH = 31
CONV_HALO = 16
RET_HEADS = 8
RET_HEAD_DIM = 64
RET_DIM = RET_HEADS * RET_HEAD_DIM
RET_CHUNK = 128
LANES = 128
SUBLANES = 8
N_PAIRS = RET_DIM // LANES
XATTN_HEADS = 4
XATTN_HEAD_DIM = D_MODEL // XATTN_HEADS
ROPE_BASE = 10000.0
NORM_EPS = 1e-6
IN_COLS = 2 * CONV_DIM + 4 * RET_DIM + 2 * D_MODEL

TOKEN_TILE = 1024
SEQ_TILE = 1024
FFN_CHUNK = 768
CONV_ROWS = 32
VMEM_LIMIT_BYTES = 58 * 1024 * 1024

F32 = jnp.float32
BF16 = jnp.bfloat16


def _const_spec(shape):
    nd = len(shape)
    return pl.BlockSpec(shape, lambda *_: (0,) * nd, pipeline_mode=pl.Buffered(1))


def _params(*semantics):
    return pltpu.CompilerParams(dimension_semantics=semantics, vmem_limit_bytes=VMEM_LIMIT_BYTES)


def _rms(x, g):
    return x * lax.rsqrt(jnp.mean(x * x, axis=-1, keepdims=True) + NORM_EPS) * g


def _dot(a, b):
    return jnp.dot(a, b, preferred_element_type=F32)


def _dot_nt(a, b):
    return lax.dot_general(a, b, (((1,), (1,)), ((), ())), preferred_element_type=F32)


def _dot_tn(a, b):
    return lax.dot_general(a, b, (((0,), (0,)), ((), ())), preferred_element_type=F32)


def _pair_log_gamma(dec_ref, p):
    return -jnp.exp(dec_ref[p:p + 1, :]), -jnp.exp(dec_ref[N_PAIRS + p:N_PAIRS + p + 1, :])


def _ffn_kernel(x_ref, gin_ref, gout_ref, wgu_ref, wd_ref, o_ref):
    x = x_ref[...]
    h = _rms(x, gin_ref[...]).astype(BF16)
    f = None
    for a in range(0, D_FF, FFN_CHUNK):
        b = min(a + FFN_CHUNK, D_FF)
        gate = _dot(h, wgu_ref[:, a:b])
        up = _dot(h, wgu_ref[:, D_FF + a:D_FF + b])
        part = _dot((gate * jax.nn.sigmoid(gate) * up).astype(BF16), wd_ref[a:b, :])
        f = part if f is None else f + part
    o_ref[...] = x + 0.5 * _rms(f, gout_ref[...])


def _ffn(x, g_in, g_out, w_gu, w_down):
    t = x.shape[0]
    tile = pl.BlockSpec((TOKEN_TILE, D_MODEL), lambda i: (i, 0))
    return pl.pallas_call(
        _ffn_kernel,
        grid=(t // TOKEN_TILE,),
        in_specs=[tile, _const_spec((1, D_MODEL)), _const_spec((1, D_MODEL)),
                  _const_spec((D_MODEL, 2 * D_FF)), _const_spec((D_FF, D_MODEL))],
        out_specs=tile,
        out_shape=jax.ShapeDtypeStruct((t, D_MODEL), F32),
        compiler_params=_params("parallel"),
        name="ffn",
    )(x, g_in, g_out, w_gu, w_down)


def _mixer_in_kernel(dec_ref, x_ref, g_ref, w_ref, gb_ref, cos_ref, sin_ref,
                     u_ref, q_ref, qf_ref, qb_ref, kf_ref, kb_ref, k_ref, v_ref, sg_ref, gate_ref):
    h = _rms(x_ref[...], g_ref[...]).astype(BF16)

    def piece(j):
        return _dot(h, w_ref[:, j * RET_DIM:(j + 1) * RET_DIM])

    ci = piece(0), piece(1)
    u_ref[...] = (ci[0] * jax.nn.sigmoid(ci[1])).astype(BF16)

    cos = cos_ref[...]
    sin = sin_ref[...]
    lane = lax.broadcasted_iota(jnp.int32, cos.shape, 1)
    first_half = (lane % RET_HEAD_DIM) < (RET_HEAD_DIM // 2)
    row = lax.broadcasted_iota(jnp.int32, (RET_CHUNK, LANES), 0).astype(F32)
    c = float(RET_CHUNK)

    def rotary(xp, scale, p):
        xc = xp[:, p * LANES:(p + 1) * LANES]
        partner = jnp.where(first_half, pltpu.roll(xc, LANES - RET_HEAD_DIM // 2, 1),
                            pltpu.roll(xc, RET_HEAD_DIM // 2, 1))
        return (xc * cos + partner * sin) * scale

    def store_decayed(x, decay, out_ref, cols):
        for n in range(x.shape[0] // RET_CHUNK):
            rows = slice(n * RET_CHUNK, (n + 1) * RET_CHUNK)
            out_ref[rows, cols] = (x[rows] * decay).astype(BF16)

    qp = piece(2)
    for p in range(N_PAIRS):
        cols = slice(p * LANES, (p + 1) * LANES)
        lgf, lgb = _pair_log_gamma(dec_ref, p)
        q = rotary(qp, 1.0, p)
        q_ref[:, cols] = q.astype(BF16)
        store_decayed(q, jnp.exp(lgf * (row + 1.0)), qf_ref, cols)
        store_decayed(q, jnp.exp(lgb * (c - row)), qb_ref, cols)
    kp = piece(3)
    for p in range(N_PAIRS):
        cols = slice(p * LANES, (p + 1) * LANES)
        lgf, lgb = _pair_log_gamma(dec_ref, p)
        k = rotary(kp, RET_HEAD_DIM ** -0.5, p)
        k_ref[:, cols] = k.astype(BF16)
        store_decayed(k, jnp.exp(lgf * (c - 1.0 - row)), kf_ref, cols)
        store_decayed(k, jnp.exp(lgb * row), kb_ref, cols)
    v_ref[...] = piece(4).astype(BF16)
    g = piece(5)
    sg_ref[...] = (g * jax.nn.sigmoid(g)).astype(BF16)
    for j in range(6, IN_COLS // RET_DIM):
        cols = slice((j - 6) * RET_DIM, (j - 5) * RET_DIM)
        gate_ref[:, cols] = jax.nn.sigmoid(piece(j) + gb_ref[:, cols]).astype(BF16)


def _mixer_in(dec, x, g, w_in, gate_b, cos, sin, seq_len):
    t = x.shape[0]
    tiles_per_seq = seq_len // TOKEN_TILE

    def rows(width):
        return pl.BlockSpec((TOKEN_TILE, width), lambda i: (i, 0))

    rope = pl.BlockSpec((TOKEN_TILE, LANES), lambda i: (i % tiles_per_seq, 0))
    out_widths = (CONV_DIM,) + (RET_DIM,) * 8 + (2 * D_MODEL,)
    return pl.pallas_call(
        _mixer_in_kernel,
        grid=(t // TOKEN_TILE,),
        in_specs=[_const_spec(dec.shape), rows(D_MODEL), _const_spec((1, D_MODEL)), _const_spec((D_MODEL, IN_COLS)),
                  _const_spec((1, 2 * D_MODEL)), rope, rope],
        out_specs=[rows(w) for w in out_widths],
        out_shape=[jax.ShapeDtypeStruct((t, w), BF16) for w in out_widths],
        compiler_params=_params("parallel"),
        name="mixer_in",
    )(dec, x, g, w_in, gate_b, cos, sin)


def _ret_state_kernel(dec_ref, kf_ref, vf_ref, kb_ref, vb_ref, sf_out, sb_out, sf_ref, sb_ref, *, n_chunks):
    @pl.when(pl.program_id(1) == 0)
    def _():
        sf_ref[...] = jnp.zeros_like(sf_ref)
        sb_ref[...] = jnp.zeros_like(sb_ref)

    ri = lax.broadcasted_iota(jnp.int32, (RET_CHUNK, LANES), 0)
    ci = lax.broadcasted_iota(jnp.int32, (RET_CHUNK, LANES), 1)
    same_head = (ri // RET_HEAD_DIM) == (ci // RET_HEAD_DIM)
    c = float(RET_CHUNK)

    def chunk_kv(k_ref, v_ref, n, p):
        rows = slice(n * RET_CHUNK, (n + 1) * RET_CHUNK)
        cols = slice(p * LANES, (p + 1) * LANES)
        return jnp.where(same_head, _dot_tn(k_ref[0, rows, cols], v_ref[0, rows, cols]), 0.0)

    for p in range(N_PAIRS):
        lgf, lgb = _pair_log_gamma(dec_ref, p)
        step_f = jnp.exp(lgf * c)
        step_b = jnp.exp(lgb * c)
        s = sf_ref[p]
        for n in range(n_chunks):
            sf_out[0, n, p] = s.astype(BF16)
            s = s * step_f + chunk_kv(kf_ref, vf_ref, n, p)
        sf_ref[p] = s
        s = sb_ref[p]
        for n in reversed(range(n_chunks)):
            sb_out[0, n, p] = s.astype(BF16)
            s = s * step_b + chunk_kv(kb_ref, vb_ref, n, p)
        sb_ref[p] = s


def _ret_state(dec, kf, kb, v):
    b, s, _ = v.shape
    nt = s // SEQ_TILE
    n_chunks = SEQ_TILE // RET_CHUNK
    fwd = pl.BlockSpec((1, SEQ_TILE, RET_DIM), lambda i, t: (i, t, 0))
    bwd = pl.BlockSpec((1, SEQ_TILE, RET_DIM), lambda i, t: (i, nt - 1 - t, 0))
    state_shape = (1, n_chunks, N_PAIRS, LANES, LANES)
    out = jax.ShapeDtypeStruct((b, s // RET_CHUNK, N_PAIRS, LANES, LANES), BF16)
    return pl.pallas_call(
        functools.partial(_ret_state_kernel, n_chunks=n_chunks),
        grid=(b, nt),
        in_specs=[_const_spec(dec.shape), fwd, fwd, bwd, bwd],
        out_specs=[pl.BlockSpec(state_shape, lambda i, t: (i, t, 0, 0, 0)),
                   pl.BlockSpec(state_shape, lambda i, t: (i, nt - 1 - t, 0, 0, 0))],
        out_shape=[out, out],
        scratch_shapes=[pltpu.VMEM((N_PAIRS, LANES, LANES), F32), pltpu.VMEM((N_PAIRS, LANES, LANES), F32)],
        compiler_params=_params("arbitrary", "arbitrary"),
        name="ret_state",
    )(dec, kf, v, kb, v)


def _mixer_kernel(dec_ref, x_ref, uprev_ref, u_ref, unext_ref, q_ref, qf_ref, qb_ref, k_ref, v_ref, sg_ref, gate_ref,
                  sf_ref, sb_ref, dww_ref, dwb_ref, lng_ref, lnb_ref, wpw_ref, gng_ref, wro_ref,
                  wmix_ref, g3_ref, pavg_ref, o_ref, upad_ref, ushift_ref, cact_ref, *, ts):
    t = pl.program_id(1)
    last = pl.num_programs(1) - 1
    n_chunks = ts // RET_CHUNK

    upad_ref[0:CONV_HALO, :] = jnp.where(t > 0, uprev_ref[0].astype(F32), 0.0)
    upad_ref[CONV_HALO:CONV_HALO + ts, :] = u_ref[0].astype(F32)
    upad_ref[CONV_HALO + ts:, :] = jnp.where(t < last, unext_ref[0].astype(F32), 0.0)
    first_tap = CONV_HALO - CONV_WIDTH // 2
    n_shift = ts + 2 * CONV_HALO - SUBLANES
    for r in range(1, SUBLANES):
        ushift_ref[r - 1] = upad_ref[r:r + n_shift, :]
    row_groups = CONV_ROWS // SUBLANES

    def conv_rows(r0, after):
        acc = jnp.concatenate([dwb_ref[...]] * row_groups, axis=0)
        if after is not None:
            bits = pltpu.bitcast(after[0:SUBLANES, 0:LANES], jnp.uint32)
            zero = pltpu.bitcast(lax.shift_right_logical(lax.shift_right_logical(bits, jnp.uint32(31)),
                                                         jnp.uint32(1)), F32)
            acc = acc + jnp.concatenate([jnp.concatenate([zero] * (CONV_DIM // LANES), axis=1)] * row_groups,
                                        axis=0)
        for w in range(CONV_WIDTH):
            r = (first_tap + w) % SUBLANES
            base = r0 + first_tap + w - r
            taps = (upad_ref[base:base + CONV_ROWS, :] if r == 0
                    else ushift_ref[r - 1, base:base + CONV_ROWS, :])
            acc = acc + taps * jnp.concatenate([dww_ref[w]] * row_groups, axis=0)
        mu = jnp.mean(acc, axis=-1, keepdims=True)
        d = acc - mu
        var = jnp.mean(d * d, axis=-1, keepdims=True)
        z = (d * lax.rsqrt(var + NORM_EPS) * jnp.concatenate([lng_ref[...]] * row_groups, axis=0)
             + jnp.concatenate([lnb_ref[...]] * row_groups, axis=0))
        cact_ref[r0:r0 + CONV_ROWS, :] = (z * jax.nn.sigmoid(z)).astype(BF16)

    ri = lax.broadcasted_iota(jnp.int32, (RET_CHUNK, RET_CHUNK), 0)
    ci = lax.broadcasted_iota(jnp.int32, (RET_CHUNK, RET_CHUNK), 1)
    ahead = jnp.maximum(ri - ci, 0).astype(F32)
    behind = jnp.maximum(ci - ri, 0).astype(F32)
    causal = ri >= ci
    head_lanes = (ci < RET_HEAD_DIM, ci >= RET_HEAD_DIM)
    decay = []
    for h in range(RET_HEADS):
        lf = -jnp.exp(dec_ref[2 * N_PAIRS + h:2 * N_PAIRS + h + 1, :])
        lb = -jnp.exp(dec_ref[2 * N_PAIRS + RET_HEADS + h:2 * N_PAIRS + RET_HEADS + h + 1, :])
        decay.append(jnp.where(causal, jnp.exp(lf * ahead), jnp.exp(lb * behind)))

    def retention_unit(n, p):
        rows = slice(n * RET_CHUNK, (n + 1) * RET_CHUNK)
        cols = slice(p * LANES, (p + 1) * LANES)
        qp = q_ref[0, rows, cols]
        vp = v_ref[0, rows, cols]
        zero = jnp.zeros_like(qp)
        scores = _dot_nt(jnp.concatenate([jnp.where(m, qp, zero) for m in head_lanes], axis=0),
                         k_ref[0, rows, cols])
        lhs = [(scores[sub * RET_CHUNK:(sub + 1) * RET_CHUNK] * decay[2 * p + sub]).astype(BF16)
               for sub in range(2)]
        lhs += [qf_ref[0, rows, cols], qb_ref[0, rows, cols]]
        rhs = [jnp.where(m, vp, zero) for m in head_lanes] + [sf_ref[0, n, p], sb_ref[0, n, p]]
        return _dot(jnp.concatenate(lhs, axis=1), jnp.concatenate(rhs, axis=0))

    def project(rows, y):
        conv_out = _dot(cact_ref[rows, :], wpw_ref[...])
        d = y - _dot(y.astype(BF16), pavg_ref[...])
        var = _dot((d * d).astype(BF16), pavg_ref[...])
        yn = d * lax.rsqrt(var + NORM_EPS) * gng_ref[...]
        ret_out = _dot((sg_ref[0, rows, :].astype(F32) * yn).astype(BF16), wro_ref[...])
        mixed = (gate_ref[0, rows, :D_MODEL].astype(F32) * conv_out
                 + gate_ref[0, rows, D_MODEL:].astype(F32) * ret_out)
        m = _dot(mixed.astype(BF16), wmix_ref[...])
        o_ref[0, rows, :] = x_ref[0, rows, :] + _rms(m, g3_ref[...])

    units = [(n, p) for n in range(n_chunks) for p in range(N_PAIRS)]
    conv_blocks = list(range(0, ts, CONV_ROWS))
    per_unit = -(-len(conv_blocks) // len(units))
    y = {}
    for i, (n, p) in enumerate(units):
        y[n, p] = retention_unit(n, p)
        for r0 in conv_blocks[i * per_unit:(i + 1) * per_unit]:
            conv_rows(r0, y[units[i - 1]] if i >= 1 else None)
    y = jnp.concatenate([jnp.concatenate([y[n, p] for p in range(N_PAIRS)], axis=1) for n in range(n_chunks)],
                        axis=0)
    project(slice(0, ts), y)


def _mixer(dec, x, u, q, qf, qb, k, v, sg, gates, sf, sb, dw_w, dw_b, ln_g, ln_b, w_pw, gn_g, w_ro, w_mix, g3,
           pavg):
    b, s, _ = x.shape
    ts = SEQ_TILE
    nt = s // ts
    n_chunks = ts // RET_CHUNK
    halo_per_tile = ts // CONV_HALO
    n_halo = s // CONV_HALO

    def seq(width):
        return pl.BlockSpec((1, ts, width), lambda i, t: (i, t, 0))

    halo_prev = pl.BlockSpec((1, CONV_HALO, CONV_DIM),
                             lambda i, t: (i, jnp.maximum(t * halo_per_tile - 1, 0), 0))
    halo_next = pl.BlockSpec((1, CONV_HALO, CONV_DIM),
                             lambda i, t: (i, jnp.minimum((t + 1) * halo_per_tile, n_halo - 1), 0))
    state = pl.BlockSpec((1, n_chunks, N_PAIRS, LANES, LANES), lambda i, t: (i, t, 0, 0, 0))
    ret = seq(RET_DIM)
    return pl.pallas_call(
        functools.partial(_mixer_kernel, ts=ts),
        grid=(b, nt),
        in_specs=[_const_spec(dec.shape), seq(D_MODEL), halo_prev, seq(CONV_DIM), halo_next,
                  ret, ret, ret, ret, ret, ret, seq(2 * D_MODEL), state, state,
                  _const_spec((CONV_WIDTH, SUBLANES, CONV_DIM)), _const_spec((SUBLANES, CONV_DIM)),
                  _const_spec((SUBLANES, CONV_DIM)), _const_spec((SUBLANES, CONV_DIM)), _const_spec((CONV_DIM, D_MODEL)), _const_spec((1, RET_DIM)),
                  _const_spec((RET_DIM, D_MODEL)), _const_spec((D_MODEL, D_MODEL)), _const_spec((1, D_MODEL)),
                  _const_spec((RET_DIM, RET_DIM))],
        out_specs=seq(D_MODEL),
        out_shape=jax.ShapeDtypeStruct((b, s, D_MODEL), F32),
        scratch_shapes=[pltpu.VMEM((ts + 2 * CONV_HALO, CONV_DIM), F32),
                        pltpu.VMEM((SUBLANES - 1, ts + 2 * CONV_HALO - SUBLANES, CONV_DIM), F32),
                        pltpu.VMEM((ts, CONV_DIM), BF16)],
        compiler_params=_params("parallel", "parallel"),
        name="mixer",
    )(dec, x, u, u, u, q, qf, qb, k, v, sg, gates, sf, sb, dw_w, dw_b, ln_g, ln_b, w_pw, gn_g, w_ro, w_mix, g3,
      pavg)


def _mem_kv_kernel(m_ref, g_ref, w_ref, o_ref):
    o_ref[...] = _dot(_rms(m_ref[...], g_ref[...]).astype(BF16), w_ref[...]).astype(BF16)


def _mem_kv(mem, g, w_kv, n_mem):
    t = mem.shape[0]
    return pl.pallas_call(
        _mem_kv_kernel,
        grid=(t // n_mem,),
        in_specs=[pl.BlockSpec((n_mem, D_MODEL), lambda i: (i, 0)), _const_spec((1, D_MODEL)),
                  _const_spec((D_MODEL, 2 * D_MODEL))],
        out_specs=pl.BlockSpec((n_mem, 2 * D_MODEL), lambda i: (i, 0)),
        out_shape=jax.ShapeDtypeStruct((t, 2 * D_MODEL), BF16),
        compiler_params=_params("parallel"),
        name="mem_kv",
    )(mem, g, w_kv)


def _xattn_kernel(x_ref, gin_ref, gout_ref, wq_ref, k_ref, v_ref, wo_ref, o_ref, att_ref):
    x = x_ref[0]
    q = _dot(_rms(x, gin_ref[...]).astype(BF16), wq_ref[...])
    for h in range(XATTN_HEADS):
        cols = slice(h * XATTN_HEAD_DIM, (h + 1) * XATTN_HEAD_DIM)
        logits = _dot_nt(q[:, cols].astype(BF16), k_ref[0, :, cols]) * (XATTN_HEAD_DIM ** -0.5)
        e = jnp.exp(logits - jnp.max(logits, axis=-1, keepdims=True))
        p = e / jnp.sum(e, axis=-1, keepdims=True)
        att_ref[:, cols] = _dot(p.astype(BF16), v_ref[0, :, cols]).astype(BF16)
    a = _dot(att_ref[...], wo_ref[...])
    o_ref[0] = x + _rms(a, gout_ref[...])


def _xattn(x, kv, g_in, g_out, w_q, w_o):
    b, s, _ = x.shape
    n_mem = kv.shape[1]
    ts = SEQ_TILE
    seq = pl.BlockSpec((1, ts, D_MODEL), lambda i, t: (i, t, 0))
    return pl.pallas_call(
        _xattn_kernel,
        grid=(b, s // ts),
        in_specs=[seq, _const_spec((1, D_MODEL)), _const_spec((1, D_MODEL)), _const_spec((D_MODEL, D_MODEL)),
                  pl.BlockSpec((1, n_mem, D_MODEL), lambda i, t: (i, 0, 0)),
                  pl.BlockSpec((1, n_mem, D_MODEL), lambda i, t: (i, 0, 1)),
                  _const_spec((D_MODEL, D_MODEL))],
        out_specs=seq,
        out_shape=jax.ShapeDtypeStruct((b, s, D_MODEL), F32),
        scratch_shapes=[pltpu.VMEM((ts, D_MODEL), BF16)],
        compiler_params=_params("parallel", "parallel"),
        name="xattn",
    )(x, g_in, g_out, w_q, kv, kv, w_o)


def _rope_tables(s):
    half = RET_HEAD_DIM // 2
    inv_freq = ROPE_BASE ** (-jnp.arange(half, dtype=F32) / half)
    ang = jnp.arange(s, dtype=F32)[:, None] * inv_freq[None, :]
    cos, sin = jnp.cos(ang), jnp.sin(ang)
    return (jnp.tile(cos, (1, LANES // half)),
            jnp.tile(jnp.concatenate([-sin, sin], axis=1), (1, LANES // RET_HEAD_DIM)))


def _decay_rows(decay_fwd, decay_bwd):
    pair = lambda d: jnp.repeat(d, RET_HEAD_DIM).reshape(N_PAIRS, LANES)
    head = lambda d: jnp.broadcast_to(d[:, None], (RET_HEADS, LANES))
    return jnp.concatenate([pair(decay_fwd), pair(decay_bwd), head(decay_fwd), head(decay_bwd)], axis=0)


def _trunk(x, mem, layers, cos, sin, pavg):
    b, s, _ = x.shape
    t = b * s
    n_mem = mem.shape[1]
    row = lambda a: a.reshape(1, -1)
    rows8 = lambda a: jnp.broadcast_to(a[..., None, :], a.shape[:-1] + (SUBLANES, a.shape[-1]))
    flat = lambda a: a.reshape(t, a.shape[-1])
    seq = lambda a: a.reshape(b, s, a.shape[-1])
    mem2 = mem.reshape(b * n_mem, D_MODEL)
    x = flat(x)
    for p in layers:
        ng = p["norm_g"]
        x = _ffn(x, row(ng[0]), row(ng[1]), p["ffn1_w_gu"], p["ffn1_w_down"])
        proj = _mixer_in(p["dec"], x, row(ng[2]), p["w_in"], row(p["gate_b"]), cos, sin, s)
        u, q, qf, qb, kf, kb, k, v, sg, gates = map(seq, proj)
        sf, sb = _ret_state(p["dec"], kf, kb, v)
        x = _mixer(p["dec"], seq(x), u, q, qf, qb, k, v, sg, gates, sf, sb, rows8(p["conv_dw_w"]),
                   rows8(p["conv_dw_b"]), rows8(p["conv_ln_g"]), rows8(p["conv_ln_b"]), p["conv_w_pw"],
                   row(p["ret_gn_g"]),
                   p["ret_w_out"], p["w_mix_out"], row(ng[3]), pavg)
        kv = _mem_kv(mem2, row(p["mem_norm_g"]), p["xattn_w_kv"], n_mem).reshape(b, n_mem, 2 * D_MODEL)
        x = _xattn(x, kv, row(ng[4]), row(ng[5]), p["xattn_w_q"], p["xattn_w_o"])
        x = _ffn(flat(x), row(ng[6]), row(ng[7]), p["ffn2_w_gu"], p["ffn2_w_down"])
    return seq(x)


def kernel(x_prompt, x_sample, mem_prompt, mem_sample, norm_g, ffn1_w_gu, ffn1_w_down, w_in, conv_dw_w, conv_dw_b,
           conv_ln_g, conv_ln_b, conv_w_pw, ret_decay_fwd, ret_decay_bwd, ret_gn_g, ret_w_out, gate_b, w_mix_out,
           mem_norm_g, xattn_w_q, xattn_w_kv, xattn_w_o, ffn2_w_gu, ffn2_w_down):
    depth = norm_g.shape[0]
    bf = lambda w: w.astype(BF16)
    layers = []
    for l in range(depth):
        layers.append(dict(
            norm_g=norm_g[l], ffn1_w_gu=bf(ffn1_w_gu[l]), ffn1_w_down=bf(ffn1_w_down[l]), w_in=bf(w_in[l]),
            conv_dw_w=conv_dw_w[l], conv_dw_b=conv_dw_b[l], conv_ln_g=conv_ln_g[l], conv_ln_b=conv_ln_b[l],
            conv_w_pw=bf(conv_w_pw[l]), dec=_decay_rows(ret_decay_fwd[l], ret_decay_bwd[l]),
            ret_gn_g=ret_gn_g[l], ret_w_out=bf(ret_w_out[l]), gate_b=gate_b[l], w_mix_out=bf(w_mix_out[l]),
            mem_norm_g=mem_norm_g[l], xattn_w_q=bf(xattn_w_q[l]), xattn_w_kv=bf(xattn_w_kv[l]),
            xattn_w_o=bf(xattn_w_o[l]), ffn2_w_gu=bf(ffn2_w_gu[l]), ffn2_w_down=bf(ffn2_w_down[l])))
    cos, sin = _rope_tables(max(x_prompt.shape[1], x_sample.shape[1]))
    head_of = jnp.arange(RET_DIM) // RET_HEAD_DIM
    pavg = jnp.where(head_of[:, None] == head_of[None, :], 1.0 / RET_HEAD_DIM, 0.0).astype(BF16)
    return (_trunk(x_prompt, mem_prompt, layers, cos, sin, pavg),
            _trunk(x_sample, mem_sample, layers, cos, sin, pavg))
```

```python
import functools

import jax
import jax.numpy as jnp
from jax import lax
from jax.experimental import pallas as pl
from jax.experimental.pallas import tpu as pltpu

D_MODEL = 1024
D_FF = 2816
CONV_DIM = 512
CONV_WIDTH = 31
CONV_HALO = 16
RET_HEADS = 8
RET_HEAD_DIM = 64
RET_DIM = RET_HEADS * RET_HEAD_DIM
RET_CHUNK = 128
LANES = 128
SUBLANES = 8
N_PAIRS = RET_DIM // LANES
XATTN_HEADS = 4
XATTN_HEAD_DIM = D_MODEL // XATTN_HEADS
ROPE_BASE = 10000.0
NORM_EPS = 1e-6
IN_COLS = 2 * CONV_DIM + 4 * RET_DIM + 2 * D_MODEL

TOKEN_TILE = 1024
SEQ_TILE = 1024
FFN_CHUNK = 768
CONV_ROWS = 32
VMEM_LIMIT_BYTES = 58 * 1024 * 1024

F32 = jnp.float32
BF16 = jnp.bfloat16


def _const_spec(shape):
    nd = len(shape)
    return pl.BlockSpec(shape, lambda *_: (0,) * nd, pipeline_mode=pl.Buffered(1))


def _layer_spec(shape, layer):
    nd = len(shape)
    return pl.BlockSpec((None,) + tuple(shape), lambda *_: (layer,) + (0,) * nd, pipeline_mode=pl.Buffered(1))


def _params(*semantics):
    return pltpu.CompilerParams(dimension_semantics=semantics, vmem_limit_bytes=VMEM_LIMIT_BYTES)


def _rms(x, g):
    return x * lax.rsqrt(jnp.mean(x * x, axis=-1, keepdims=True) + NORM_EPS) * g


def _dot(a, b):
    return jnp.dot(a, b, preferred_element_type=F32)


def _dot_nt(a, b):
    return lax.dot_general(a, b, (((1,), (1,)), ((), ())), preferred_element_type=F32)


def _dot_tn(a, b):
    return lax.dot_general(a, b, (((0,), (0,)), ((), ())), preferred_element_type=F32)


def _pair_log_gamma(dec_ref, p):
    return -jnp.exp(dec_ref[p:p + 1, :]), -jnp.exp(dec_ref[N_PAIRS + p:N_PAIRS + p + 1, :])


def _ffn_kernel(x_ref, gin_ref, gout_ref, wgu_ref, wd_ref, o_ref):
    x = x_ref[...]
    h = _rms(x, gin_ref[...]).astype(BF16)
    f = None
    for a in range(0, D_FF, FFN_CHUNK):
        b = min(a + FFN_CHUNK, D_FF)
        gate = _dot(h, wgu_ref[:, a:b])
        up = _dot(h, wgu_ref[:, D_FF + a:D_FF + b])
        part = _dot((gate * jax.nn.sigmoid(gate) * up).astype(BF16), wd_ref[a:b, :])
        f = part if f is None else f + part
    o_ref[...] = x + 0.5 * _rms(f, gout_ref[...])


def _ffn(x, g_in, g_out, w_gu, w_down, layer):
    t = x.shape[0]
    tile = pl.BlockSpec((TOKEN_TILE, D_MODEL), lambda i: (i, 0))
    return pl.pallas_call(
        _ffn_kernel,
        grid=(t // TOKEN_TILE,),
        in_specs=[tile, _const_spec((1, D_MODEL)), _const_spec((1, D_MODEL)),
                  _layer_spec((D_MODEL, 2 * D_FF), layer), _layer_spec((D_FF, D_MODEL), layer)],
        out_specs=tile,
        out_shape=jax.ShapeDtypeStruct((t, D_MODEL), F32),
        compiler_params=_params("parallel"),
        name="ffn",
    )(x, g_in, g_out, w_gu, w_down)


def _mixer_in_kernel(dec_ref, x_ref, g_ref, w_ref, gb_ref, cos_ref, sin_ref,
                     u_ref, q_ref, qf_ref, qb_ref, kf_ref, kb_ref, k_ref, v_ref, sg_ref, gate_ref):
    h = _rms(x_ref[...], g_ref[...]).astype(BF16)

    def piece(j):
        return _dot(h, w_ref[:, j * RET_DIM:(j + 1) * RET_DIM])

    ci = piece(0), piece(1)
    u_ref[...] = (ci[0] * jax.nn.sigmoid(ci[1])).astype(BF16)

    cos = cos_ref[...]
    sin = sin_ref[...]
    lane = lax.broadcasted_iota(jnp.int32, cos.shape, 1)
    first_half = (lane % RET_HEAD_DIM) < (RET_HEAD_DIM // 2)
    row = lax.broadcasted_iota(jnp.int32, (RET_CHUNK, LANES), 0).astype(F32)
    c = float(RET_CHUNK)

    def rotary(xp, scale, p):
        xc = xp[:, p * LANES:(p + 1) * LANES]
        partner = jnp.where(first_half, pltpu.roll(xc, LANES - RET_HEAD_DIM // 2, 1),
                            pltpu.roll(xc, RET_HEAD_DIM // 2, 1))
        return (xc * cos + partner * sin) * scale

    def store_decayed(x, decay, out_ref, cols):
        for n in range(x.shape[0] // RET_CHUNK):
            rows = slice(n * RET_CHUNK, (n + 1) * RET_CHUNK)
            out_ref[rows, cols] = (x[rows] * decay).astype(BF16)

    qp = piece(2)
    for p in range(N_PAIRS):
        cols = slice(p * LANES, (p + 1) * LANES)
        lgf, lgb = _pair_log_gamma(dec_ref, p)
        q = rotary(qp, 1.0, p)
        q_ref[:, cols] = q.astype(BF16)
        store_decayed(q, jnp.exp(lgf * (row + 1.0)), qf_ref, cols)
        store_decayed(q, jnp.exp(lgb * (c - row)), qb_ref, cols)
    kp = piece(3)
    for p in range(N_PAIRS):
        cols = slice(p * LANES, (p + 1) * LANES)
        lgf, lgb = _pair_log_gamma(dec_ref, p)
        k = rotary(kp, RET_HEAD_DIM ** -0.5, p)
        k_ref[:, cols] = k.astype(BF16)
        store_decayed(k, jnp.exp(lgf * (c - 1.0 - row)), kf_ref, cols)
        store_decayed(k, jnp.exp(lgb * row), kb_ref, cols)
    v_ref[...] = piece(4).astype(BF16)
    g = piece(5)
    sg_ref[...] = (g * jax.nn.sigmoid(g)).astype(BF16)
    for j in range(6, IN_COLS // RET_DIM):
        cols = slice((j - 6) * RET_DIM, (j - 5) * RET_DIM)
        gate_ref[:, cols] = jax.nn.sigmoid(piece(j) + gb_ref[:, cols]).astype(BF16)


def _mixer_in(dec, x, g, w_in, gate_b, cos, sin, seq_len, layer):
    t = x.shape[0]
    tiles_per_seq = seq_len // TOKEN_TILE

    def rows(width):
        return pl.BlockSpec((TOKEN_TILE, width), lambda i: (i, 0))

    rope = pl.BlockSpec((TOKEN_TILE, LANES), lambda i: (i % tiles_per_seq, 0))
    out_widths = (CONV_DIM,) + (RET_DIM,) * 8 + (2 * D_MODEL,)
    return pl.pallas_call(
        _mixer_in_kernel,
        grid=(t // TOKEN_TILE,),
        in_specs=[_const_spec(dec.shape), rows(D_MODEL), _const_spec((1, D_MODEL)),
                  _layer_spec((D_MODEL, IN_COLS), layer), _const_spec((1, 2 * D_MODEL)), rope, rope],
        out_specs=[rows(w) for w in out_widths],
        out_shape=[jax.ShapeDtypeStruct((t, w), BF16) for w in out_widths],
        compiler_params=_params("parallel"),
        name="mixer_in",
    )(dec, x, g, w_in, gate_b, cos, sin)


def _ret_state_kernel(dec_ref, kf_ref, vf_ref, kb_ref, vb_ref, sf_out, sb_out, sf_ref, sb_ref, *, n_chunks):
    @pl.when(pl.program_id(1) == 0)
    def _():
        sf_ref[...] = jnp.zeros_like(sf_ref)
        sb_ref[...] = jnp.zeros_like(sb_ref)

    ri = lax.broadcasted_iota(jnp.int32, (RET_CHUNK, LANES), 0)
    ci = lax.broadcasted_iota(jnp.int32, (RET_CHUNK, LANES), 1)
    same_head = (ri // RET_HEAD_DIM) == (ci // RET_HEAD_DIM)
    c = float(RET_CHUNK)

    def chunk_kv(k_ref, v_ref, n, p):
        rows = slice(n * RET_CHUNK, (n + 1) * RET_CHUNK)
        cols = slice(p * LANES, (p + 1) * LANES)
        return jnp.where(same_head, _dot_tn(k_ref[0, rows, cols], v_ref[0, rows, cols]), 0.0)

    for p in range(N_PAIRS):
        lgf, lgb = _pair_log_gamma(dec_ref, p)
        step_f = jnp.exp(lgf * c)
        step_b = jnp.exp(lgb * c)
        s = sf_ref[p]
        for n in range(n_chunks):
            sf_out[0, n, p] = s.astype(BF16)
            s = s * step_f + chunk_kv(kf_ref, vf_ref, n, p)
        sf_ref[p] = s
        s = sb_ref[p]
        for n in reversed(range(n_chunks)):
            sb_out[0, n, p] = s.astype(BF16)
            s = s * step_b + chunk_kv(kb_ref, vb_ref, n, p)
        sb_ref[p] = s


def _ret_state(dec, kf, kb, v):
    b, s, _ = v.shape
    nt = s // SEQ_TILE
    n_chunks = SEQ_TILE // RET_CHUNK
    fwd = pl.BlockSpec((1, SEQ_TILE, RET_DIM), lambda i, t: (i, t, 0))
    bwd = pl.BlockSpec((1, SEQ_TILE, RET_DIM), lambda i, t: (i, nt - 1 - t, 0))
    state_shape = (1, n_chunks, N_PAIRS, LANES, LANES)
    out = jax.ShapeDtypeStruct((b, s // RET_CHUNK, N_PAIRS, LANES, LANES), BF16)
    return pl.pallas_call(
        functools.partial(_ret_state_kernel, n_chunks=n_chunks),
        grid=(b, nt),
        in_specs=[_const_spec(dec.shape), fwd, fwd, bwd, bwd],
        out_specs=[pl.BlockSpec(state_shape, lambda i, t: (i, t, 0, 0, 0)),
                   pl.BlockSpec(state_shape, lambda i, t: (i, nt - 1 - t, 0, 0, 0))],
        out_shape=[out, out],
        scratch_shapes=[pltpu.VMEM((N_PAIRS, LANES, LANES), F32), pltpu.VMEM((N_PAIRS, LANES, LANES), F32)],
        compiler_params=_params("arbitrary", "arbitrary"),
        name="ret_state",
    )(dec, kf, v, kb, v)


def _mixer_kernel(dec_ref, x_ref, uprev_ref, u_ref, unext_ref, q_ref, qf_ref, qb_ref, k_ref, v_ref, sg_ref, gate_ref,
                  sf_ref, sb_ref, dww_ref, dwb_ref, lng_ref, lnb_ref, wpw_ref, gng_ref, wro_ref,
                  wmix_ref, g3_ref, pavg_ref, o_ref, upad_ref, ushift_ref, cact_ref, *, ts):
    t = pl.program_id(1)
    last = pl.num_programs(1) - 1
    n_chunks = ts // RET_CHUNK

    upad_ref[0:CONV_HALO, :] = jnp.where(t > 0, uprev_ref[0].astype(F32), 0.0)
    upad_ref[CONV_HALO:CONV_HALO + ts, :] = u_ref[0].astype(F32)
    upad_ref[CONV_HALO + ts:, :] = jnp.where(t < last, unext_ref[0].astype(F32), 0.0)
    first_tap = CONV_HALO - CONV_WIDTH // 2
    n_shift = ts + 2 * CONV_HALO - SUBLANES
    for r in range(1, SUBLANES):
        ushift_ref[r - 1] = upad_ref[r:r + n_shift, :]
    row_groups = CONV_ROWS // SUBLANES

    def conv_rows(r0, after):
        acc = jnp.concatenate([dwb_ref[...]] * row_groups, axis=0)
        if after is not None:
            bits = pltpu.bitcast(after[0:SUBLANES, 0:LANES], jnp.uint32)
            zero = pltpu.bitcast(lax.shift_right_logical(lax.shift_right_logical(bits, jnp.uint32(31)),
                                                         jnp.uint32(1)), F32)
            acc = acc + jnp.concatenate([jnp.concatenate([zero] * (CONV_DIM // LANES), axis=1)] * row_groups,
                                        axis=0)
        for w in range(CONV_WIDTH):
            r = (first_tap + w) % SUBLANES
            base = r0 + first_tap + w - r
            taps = (upad_ref[base:base + CONV_ROWS, :] if r == 0
                    else ushift_ref[r - 1, base:base + CONV_ROWS, :])
            acc = acc + taps * jnp.concatenate([dww_ref[w]] * row_groups, axis=0)
        mu = jnp.mean(acc, axis=-1, keepdims=True)
        d = acc - mu
        var = jnp.mean(d * d, axis=-1, keepdims=True)
        z = (d * lax.rsqrt(var + NORM_EPS) * jnp.concatenate([lng_ref[...]] * row_groups, axis=0)
             + jnp.concatenate([lnb_ref[...]] * row_groups, axis=0))
        cact_ref[r0:r0 + CONV_ROWS, :] = (z * jax.nn.sigmoid(z)).astype(BF16)

    ri = lax.broadcasted_iota(jnp.int32, (RET_CHUNK, RET_CHUNK), 0)
    ci = lax.broadcasted_iota(jnp.int32, (RET_CHUNK, RET_CHUNK), 1)
    ahead = jnp.maximum(ri - ci, 0).astype(F32)
    behind = jnp.maximum(ci - ri, 0).astype(F32)
    causal = ri >= ci
    head_lanes = (ci < RET_HEAD_DIM, ci >= RET_HEAD_DIM)
    decay = []
    for h in range(RET_HEADS):
        lf = -jnp.exp(dec_ref[2 * N_PAIRS + h:2 * N_PAIRS + h + 1, :])
        lb = -jnp.exp(dec_ref[2 * N_PAIRS + RET_HEADS + h:2 * N_PAIRS + RET_HEADS + h + 1, :])
        decay.append(jnp.where(causal, jnp.exp(lf * ahead), jnp.exp(lb * behind)))

    def retention_unit(n, p):
        rows = slice(n * RET_CHUNK, (n + 1) * RET_CHUNK)
        cols = slice(p * LANES, (p + 1) * LANES)
        qp = q_ref[0, rows, cols]
        vp = v_ref[0, rows, cols]
        zero = jnp.zeros_like(qp)
        scores = _dot_nt(jnp.concatenate([jnp.where(m, qp, zero) for m in head_lanes], axis=0),
                         k_ref[0, rows, cols])
        lhs = [(scores[sub * RET_CHUNK:(sub + 1) * RET_CHUNK] * decay[2 * p + sub]).astype(BF16)
               for sub in range(2)]
        lhs += [qf_ref[0, rows, cols], qb_ref[0, rows, cols]]
        rhs = [jnp.where(m, vp, zero) for m in head_lanes] + [sf_ref[0, n, p], sb_ref[0, n, p]]
        return _dot(jnp.concatenate(lhs, axis=1), jnp.concatenate(rhs, axis=0))

    def project(rows, y):
        conv_out = _dot(cact_ref[rows, :], wpw_ref[...])
        d = y - _dot(y.astype(BF16), pavg_ref[...])
        var = _dot((d * d).astype(BF16), pavg_ref[...])
        yn = d * lax.rsqrt(var + NORM_EPS) * gng_ref[...]
        ret_out = _dot((sg_ref[0, rows, :].astype(F32) * yn).astype(BF16), wro_ref[...])
        mixed = (gate_ref[0, rows, :D_MODEL].astype(F32) * conv_out
                 + gate_ref[0, rows, D_MODEL:].astype(F32) * ret_out)
        m = _dot(mixed.astype(BF16), wmix_ref[...])
        o_ref[0, rows, :] = x_ref[0, rows, :] + _rms(m, g3_ref[...])

    units = [(n, p) for n in range(n_chunks) for p in range(N_PAIRS)]
    conv_blocks = list(range(0, ts, CONV_ROWS))
    per_unit = -(-len(conv_blocks) // len(units))
    y = {}
    for i, (n, p) in enumerate(units):
        y[n, p] = retention_unit(n, p)
        for r0 in conv_blocks[i * per_unit:(i + 1) * per_unit]:
            conv_rows(r0, y[units[i - 1]] if i >= 1 else None)
    y = jnp.concatenate([jnp.concatenate([y[n, p] for p in range(N_PAIRS)], axis=1) for n in range(n_chunks)],
                        axis=0)
    project(slice(0, ts), y)


def _mixer(dec, x, u, q, qf, qb, k, v, sg, gates, sf, sb, dw_w, dw_b, ln_g, ln_b, w_pw, gn_g, w_ro, w_mix, g3,
           pavg, layer):
    b, s, _ = x.shape
    ts = SEQ_TILE
    nt = s // ts
    n_chunks = ts // RET_CHUNK
    halo_per_tile = ts // CONV_HALO
    n_halo = s // CONV_HALO

    def seq(width):
        return pl.BlockSpec((1, ts, width), lambda i, t: (i, t, 0))

    halo_prev = pl.BlockSpec((1, CONV_HALO, CONV_DIM),
                             lambda i, t: (i, jnp.maximum(t * halo_per_tile - 1, 0), 0))
    halo_next = pl.BlockSpec((1, CONV_HALO, CONV_DIM),
                             lambda i, t: (i, jnp.minimum((t + 1) * halo_per_tile, n_halo - 1), 0))
    state = pl.BlockSpec((1, n_chunks, N_PAIRS, LANES, LANES), lambda i, t: (i, t, 0, 0, 0))
    ret = seq(RET_DIM)
    return pl.pallas_call(
        functools.partial(_mixer_kernel, ts=ts),
        grid=(b, nt),
        in_specs=[_const_spec(dec.shape), seq(D_MODEL), halo_prev, seq(CONV_DIM), halo_next,
                  ret, ret, ret, ret, ret, ret, seq(2 * D_MODEL), state, state,
                  _const_spec((CONV_WIDTH, SUBLANES, CONV_DIM)), _const_spec((SUBLANES, CONV_DIM)),
                  _const_spec((SUBLANES, CONV_DIM)), _const_spec((SUBLANES, CONV_DIM)),
                  _layer_spec((CONV_DIM, D_MODEL), layer), _const_spec((1, RET_DIM)),
                  _layer_spec((RET_DIM, D_MODEL), layer), _layer_spec((D_MODEL, D_MODEL), layer),
                  _const_spec((1, D_MODEL)), _const_spec((RET_DIM, RET_DIM))],
        out_specs=seq(D_MODEL),
        out_shape=jax.ShapeDtypeStruct((b, s, D_MODEL), F32),
        scratch_shapes=[pltpu.VMEM((ts + 2 * CONV_HALO, CONV_DIM), F32),
                        pltpu.VMEM((SUBLANES - 1, ts + 2 * CONV_HALO - SUBLANES, CONV_DIM), F32),
                        pltpu.VMEM((ts, CONV_DIM), BF16)],
        compiler_params=_params("parallel", "parallel"),
        name="mixer",
    )(dec, x, u, u, u, q, qf, qb, k, v, sg, gates, sf, sb, dw_w, dw_b, ln_g, ln_b, w_pw, gn_g, w_ro, w_mix, g3,
      pavg)


def _mem_kv_kernel(m_ref, g_ref, w_ref, o_ref):
    o_ref[...] = _dot(_rms(m_ref[...], g_ref[...]).astype(BF16), w_ref[...]).astype(BF16)


def _mem_kv(mem, g, w_kv, n_mem, layer):
    t = mem.shape[0]
    return pl.pallas_call(
        _mem_kv_kernel,
        grid=(t // n_mem,),
        in_specs=[pl.BlockSpec((n_mem, D_MODEL), lambda i: (i, 0)), _const_spec((1, D_MODEL)),
                  _layer_spec((D_MODEL, 2 * D_MODEL), layer)],
        out_specs=pl.BlockSpec((n_mem, 2 * D_MODEL), lambda i: (i, 0)),
        out_shape=jax.ShapeDtypeStruct((t, 2 * D_MODEL), BF16),
        compiler_params=_params("parallel"),
        name="mem_kv",
    )(mem, g, w_kv)


def _xattn_kernel(x_ref, gin_ref, gout_ref, wq_ref, k_ref, v_ref, wo_ref, o_ref, att_ref):
    x = x_ref[0]
    q = _dot(_rms(x, gin_ref[...]).astype(BF16), wq_ref[...])
    for h in range(XATTN_HEADS):
        cols = slice(h * XATTN_HEAD_DIM, (h + 1) * XATTN_HEAD_DIM)
        logits = _dot_nt(q[:, cols].astype(BF16), k_ref[0, :, cols]) * (XATTN_HEAD_DIM ** -0.5)
        e = jnp.exp(logits - jnp.max(logits, axis=-1, keepdims=True))
        p = e / jnp.sum(e, axis=-1, keepdims=True)
        att_ref[:, cols] = _dot(p.astype(BF16), v_ref[0, :, cols]).astype(BF16)
    a = _dot(att_ref[...], wo_ref[...])
    o_ref[0] = x + _rms(a, gout_ref[...])


def _xattn(x, kv, g_in, g_out, w_q, w_o, layer):
    b, s, _ = x.shape
    n_mem = kv.shape[1]
    ts = SEQ_TILE
    seq = pl.BlockSpec((1, ts, D_MODEL), lambda i, t: (i, t, 0))
    return pl.pallas_call(
        _xattn_kernel,
        grid=(b, s // ts),
        in_specs=[seq, _const_spec((1, D_MODEL)), _const_spec((1, D_MODEL)),
                  _layer_spec((D_MODEL, D_MODEL), layer),
                  pl.BlockSpec((1, n_mem, D_MODEL), lambda i, t: (i, 0, 0)),
                  pl.BlockSpec((1, n_mem, D_MODEL), lambda i, t: (i, 0, 1)),
                  _layer_spec((D_MODEL, D_MODEL), layer)],
        out_specs=seq,
        out_shape=jax.ShapeDtypeStruct((b, s, D_MODEL), F32),
        scratch_shapes=[pltpu.VMEM((ts, D_MODEL), BF16)],
        compiler_params=_params("parallel", "parallel"),
        name="xattn",
    )(x, g_in, g_out, w_q, kv, kv, w_o)


def _rope_tables(s):
    half = RET_HEAD_DIM // 2
    inv_freq = ROPE_BASE ** (-jnp.arange(half, dtype=F32) / half)
    ang = jnp.arange(s, dtype=F32)[:, None] * inv_freq[None, :]
    cos, sin = jnp.cos(ang), jnp.sin(ang)
    return (jnp.tile(cos, (1, LANES // half)),
            jnp.tile(jnp.concatenate([-sin, sin], axis=1), (1, LANES // RET_HEAD_DIM)))


def _decay_rows(decay_fwd, decay_bwd):
    pair = lambda d: jnp.repeat(d, RET_HEAD_DIM).reshape(N_PAIRS, LANES)
    head = lambda d: jnp.broadcast_to(d[:, None], (RET_HEADS, LANES))
    return jnp.concatenate([pair(decay_fwd), pair(decay_bwd), head(decay_fwd), head(decay_bwd)], axis=0)


def _trunk(x, mem, w, small, cos, sin, pavg):
    b, s, _ = x.shape
    t = b * s
    n_mem = mem.shape[1]
    row = lambda a: a.reshape(1, -1)
    rows8 = lambda a: jnp.broadcast_to(a[..., None, :], a.shape[:-1] + (SUBLANES, a.shape[-1]))
    flat = lambda a: a.reshape(t, a.shape[-1])
    seq = lambda a: a.reshape(b, s, a.shape[-1])
    mem2 = mem.reshape(b * n_mem, D_MODEL)
    x = flat(x)
    for l, p in enumerate(small):
        ng = p["norm_g"]
        x = _ffn(x, row(ng[0]), row(ng[1]), w["ffn1_w_gu"], w["ffn1_w_down"], l)
        proj = _mixer_in(p["dec"], x, row(ng[2]), w["w_in"], row(p["gate_b"]), cos, sin, s, l)
        u, q, qf, qb, kf, kb, k, v, sg, gates = map(seq, proj)
        sf, sb = _ret_state(p["dec"], kf, kb, v)
        x = _mixer(p["dec"], seq(x), u, q, qf, qb, k, v, sg, gates, sf, sb, rows8(p["conv_dw_w"]),
                   rows8(p["conv_dw_b"]), rows8(p["conv_ln_g"]), rows8(p["conv_ln_b"]), w["conv_w_pw"],
                   row(p["ret_gn_g"]), w["ret_w_out"], w["w_mix_out"], row(ng[3]), pavg, l)
        kv = _mem_kv(mem2, row(p["mem_norm_g"]), w["xattn_w_kv"], n_mem, l).reshape(b, n_mem, 2 * D_MODEL)
        x = _xattn(x, kv, row(ng[4]), row(ng[5]), w["xattn_w_q"], w["xattn_w_o"], l)
        x = _ffn(flat(x), row(ng[6]), row(ng[7]), w["ffn2_w_gu"], w["ffn2_w_down"], l)
    return seq(x)


def kernel(x_prompt, x_sample, mem_prompt, mem_sample, norm_g, ffn1_w_gu, ffn1_w_down, w_in, conv_dw_w, conv_dw_b,
           conv_ln_g, conv_ln_b, conv_w_pw, ret_decay_fwd, ret_decay_bwd, ret_gn_g, ret_w_out, gate_b, w_mix_out,
           mem_norm_g, xattn_w_q, xattn_w_kv, xattn_w_o, ffn2_w_gu, ffn2_w_down):
    w = {name: a.astype(BF16) for name, a in dict(
        ffn1_w_gu=ffn1_w_gu, ffn1_w_down=ffn1_w_down, w_in=w_in, conv_w_pw=conv_w_pw, ret_w_out=ret_w_out,
        w_mix_out=w_mix_out, xattn_w_q=xattn_w_q, xattn_w_kv=xattn_w_kv, xattn_w_o=xattn_w_o,
        ffn2_w_gu=ffn2_w_gu, ffn2_w_down=ffn2_w_down).items()}
    small = [dict(norm_g=norm_g[l], conv_dw_w=conv_dw_w[l], conv_dw_b=conv_dw_b[l], conv_ln_g=conv_ln_g[l],
                  conv_ln_b=conv_ln_b[l], dec=_decay_rows(ret_decay_fwd[l], ret_decay_bwd[l]), ret_gn_g=ret_gn_g[l],
                  gate_b=gate_b[l], mem_norm_g=mem_norm_g[l]) for l in range(norm_g.shape[0])]
    cos, sin = _rope_tables(max(x_prompt.shape[1], x_sample.shape[1]))
    head_of = jnp.arange(RET_DIM) // RET_HEAD_DIM
    pavg = jnp.where(head_of[:, None] == head_of[None, :], 1.0 / RET_HEAD_DIM, 0.0).astype(BF16)
    return (_trunk(x_prompt, mem_prompt, w, small, cos, sin, pavg),
            _trunk(x_sample, mem_sample, w, small, cos, sin, pavg))
```

```python
import functools

import jax
import jax.numpy as jnp
from jax import lax
from jax.experimental import pallas as pl
from jax.experimental.pallas import tpu as pltpu

D_MODEL = 1024
D_FF = 2816
CONV_DIM = 512
CONV_WIDTH = 31
CONV_HALO = 16
RET_HEADS = 8
RET_HEAD_DIM = 64
RET_DIM = RET_HEADS * RET_HEAD_DIM
RET_CHUNK = 128
LANES = 128
SUBLANES = 8
N_PAIRS = RET_DIM // LANES
XATTN_HEADS = 4
XATTN_HEAD_DIM = D_MODEL // XATTN_HEADS
ROPE_BASE = 10000.0
NORM_EPS = 1e-6
IN_COLS = 2 * CONV_DIM + 4 * RET_DIM + 2 * D_MODEL

TOKEN_TILE = 1024
SEQ_TILE = 1024
FFN_CHUNK = 1024
XATTN_TILE = 2048
CONV_ROWS = 32
VMEM_LIMIT_BYTES = 58 * 1024 * 1024

F32 = jnp.float32
BF16 = jnp.bfloat16


def _const_spec(shape):
    nd = len(shape)
    return pl.BlockSpec(shape, lambda *_: (0,) * nd, pipeline_mode=pl.Buffered(1))


def _layer_spec(shape, layer):
    nd = len(shape)
    return pl.BlockSpec((None,) + tuple(shape), lambda *_: (layer,) + (0,) * nd, pipeline_mode=pl.Buffered(1))


def _params(*semantics):
    return pltpu.CompilerParams(dimension_semantics=semantics, vmem_limit_bytes=VMEM_LIMIT_BYTES)


def _rms(x, g):
    return x * lax.rsqrt(jnp.mean(x * x, axis=-1, keepdims=True) + NORM_EPS) * g


def _dot(a, b):
    return jnp.dot(a, b, preferred_element_type=F32)


def _dot_nt(a, b):
    return lax.dot_general(a, b, (((1,), (1,)), ((), ())), preferred_element_type=F32)


def _dot_tn(a, b):
    return lax.dot_general(a, b, (((0,), (0,)), ((), ())), preferred_element_type=F32)


def _pair_log_gamma(dec_ref, p):
    return -jnp.exp(dec_ref[p:p + 1, :]), -jnp.exp(dec_ref[N_PAIRS + p:N_PAIRS + p + 1, :])


def _ffn_kernel(x_ref, gin_ref, gout_ref, wgu_ref, wd_ref, o_ref):
    x = x_ref[...]
    h = _rms(x, gin_ref[...]).astype(BF16)
    f = None
    for a in range(0, D_FF, FFN_CHUNK):
        b = min(a + FFN_CHUNK, D_FF)
        gate = _dot(h, wgu_ref[:, a:b])
        up = _dot(h, wgu_ref[:, D_FF + a:D_FF + b])
        part = _dot((gate * jax.nn.sigmoid(gate) * up).astype(BF16), wd_ref[a:b, :])
        f = part if f is None else f + part
    o_ref[...] = x + 0.5 * _rms(f, gout_ref[...])


def _ffn(x, g_in, g_out, w_gu, w_down, layer):
    t = x.shape[0]
    tile = pl.BlockSpec((TOKEN_TILE, D_MODEL), lambda i: (i, 0))
    return pl.pallas_call(
        _ffn_kernel,
        grid=(t // TOKEN_TILE,),
        in_specs=[tile, _const_spec((1, D_MODEL)), _const_spec((1, D_MODEL)),
                  _layer_spec((D_MODEL, 2 * D_FF), layer), _layer_spec((D_FF, D_MODEL), layer)],
        out_specs=tile,
        out_shape=jax.ShapeDtypeStruct((t, D_MODEL), F32),
        compiler_params=_params("parallel"),
        name="ffn",
    )(x, g_in, g_out, w_gu, w_down)


def _mixer_in_kernel(dec_ref, x_ref, g_ref, w_ref, gb_ref, cos_ref, sin_ref,
                     u_ref, q_ref, qf_ref, qb_ref, kf_ref, kb_ref, k_ref, v_ref, sg_ref, gate_ref):
    h = _rms(x_ref[...], g_ref[...]).astype(BF16)

    def piece(j):
        return _dot(h, w_ref[:, j * RET_DIM:(j + 1) * RET_DIM])

    ci = piece(0), piece(1)
    u_ref[...] = (ci[0] * jax.nn.sigmoid(ci[1])).astype(BF16)

    cos = cos_ref[...]
    sin = sin_ref[...]
    lane = lax.broadcasted_iota(jnp.int32, cos.shape, 1)
    first_half = (lane % RET_HEAD_DIM) < (RET_HEAD_DIM // 2)
    row = lax.broadcasted_iota(jnp.int32, (RET_CHUNK, LANES), 0).astype(F32)
    c = float(RET_CHUNK)

    def rotary(xp, scale, p):
        xc = xp[:, p * LANES:(p + 1) * LANES]
        partner = jnp.where(first_half, pltpu.roll(xc, LANES - RET_HEAD_DIM // 2, 1),
                            pltpu.roll(xc, RET_HEAD_DIM // 2, 1))
        return (xc * cos + partner * sin) * scale

    def store_decayed(x, decay, out_ref, cols):
        for n in range(x.shape[0] // RET_CHUNK):
            rows = slice(n * RET_CHUNK, (n + 1) * RET_CHUNK)
            out_ref[rows, cols] = (x[rows] * decay).astype(BF16)

    qp = piece(2)
    for p in range(N_PAIRS):
        cols = slice(p * LANES, (p + 1) * LANES)
        lgf, lgb = _pair_log_gamma(dec_ref, p)
        q = rotary(qp, 1.0, p)
        q_ref[:, cols] = q.astype(BF16)
        store_decayed(q, jnp.exp(lgf * (row + 1.0)), qf_ref, cols)
        store_decayed(q, jnp.exp(lgb * (c - row)), qb_ref, cols)
    kp = piece(3)
    for p in range(N_PAIRS):
        cols = slice(p * LANES, (p + 1) * LANES)
        lgf, lgb = _pair_log_gamma(dec_ref, p)
        k = rotary(kp, RET_HEAD_DIM ** -0.5, p)
        k_ref[:, cols] = k.astype(BF16)
        store_decayed(k, jnp.exp(lgf * (c - 1.0 - row)), kf_ref, cols)
        store_decayed(k, jnp.exp(lgb * row), kb_ref, cols)
    v_ref[...] = piece(4).astype(BF16)
    g = piece(5)
    sg_ref[...] = (g * jax.nn.sigmoid(g)).astype(BF16)
    for j in range(6, IN_COLS // RET_DIM):
        cols = slice((j - 6) * RET_DIM, (j - 5) * RET_DIM)
        gate_ref[:, cols] = jax.nn.sigmoid(piece(j) + gb_ref[:, cols]).astype(BF16)


def _mixer_in(dec, x, g, w_in, gate_b, cos, sin, seq_len, layer):
    t = x.shape[0]
    tiles_per_seq = seq_len // TOKEN_TILE

    def rows(width):
        return pl.BlockSpec((TOKEN_TILE, width), lambda i: (i, 0))

    rope = pl.BlockSpec((TOKEN_TILE, LANES), lambda i: (i % tiles_per_seq, 0))
    out_widths = (CONV_DIM,) + (RET_DIM,) * 8 + (2 * D_MODEL,)
    return pl.pallas_call(
        _mixer_in_kernel,
        grid=(t // TOKEN_TILE,),
        in_specs=[_const_spec(dec.shape), rows(D_MODEL), _const_spec((1, D_MODEL)),
                  _layer_spec((D_MODEL, IN_COLS), layer), _const_spec((1, 2 * D_MODEL)), rope, rope],
        out_specs=[rows(w) for w in out_widths],
        out_shape=[jax.ShapeDtypeStruct((t, w), BF16) for w in out_widths],
        compiler_params=_params("parallel"),
        name="mixer_in",
    )(dec, x, g, w_in, gate_b, cos, sin)


def _ret_state_kernel(dec_ref, kf_ref, vf_ref, kb_ref, vb_ref, sf_out, sb_out, sf_ref, sb_ref, *, n_chunks):
    @pl.when(pl.program_id(1) == 0)
    def _():
        sf_ref[...] = jnp.zeros_like(sf_ref)
        sb_ref[...] = jnp.zeros_like(sb_ref)

    ri = lax.broadcasted_iota(jnp.int32, (RET_CHUNK, LANES), 0)
    ci = lax.broadcasted_iota(jnp.int32, (RET_CHUNK, LANES), 1)
    same_head = (ri // RET_HEAD_DIM) == (ci // RET_HEAD_DIM)
    c = float(RET_CHUNK)

    def chunk_kv(k_ref, v_ref, n, p):
        rows = slice(n * RET_CHUNK, (n + 1) * RET_CHUNK)
        cols = slice(p * LANES, (p + 1) * LANES)
        return jnp.where(same_head, _dot_tn(k_ref[0, rows, cols], v_ref[0, rows, cols]), 0.0)

    for p in range(N_PAIRS):
        lgf, lgb = _pair_log_gamma(dec_ref, p)
        step_f = jnp.exp(lgf * c)
        step_b = jnp.exp(lgb * c)
        s = sf_ref[p]
        for n in range(n_chunks):
            sf_out[0, n, p] = s.astype(BF16)
            s = s * step_f + chunk_kv(kf_ref, vf_ref, n, p)
        sf_ref[p] = s
        s = sb_ref[p]
        for n in reversed(range(n_chunks)):
            sb_out[0, n, p] = s.astype(BF16)
            s = s * step_b + chunk_kv(kb_ref, vb_ref, n, p)
        sb_ref[p] = s


def _ret_state(dec, kf, kb, v):
    b, s, _ = v.shape
    nt = s // SEQ_TILE
    n_chunks = SEQ_TILE // RET_CHUNK
    fwd = pl.BlockSpec((1, SEQ_TILE, RET_DIM), lambda i, t: (i, t, 0))
    bwd = pl.BlockSpec((1, SEQ_TILE, RET_DIM), lambda i, t: (i, nt - 1 - t, 0))
    state_shape = (1, n_chunks, N_PAIRS, LANES, LANES)
    out = jax.ShapeDtypeStruct((b, s // RET_CHUNK, N_PAIRS, LANES, LANES), BF16)
    return pl.pallas_call(
        functools.partial(_ret_state_kernel, n_chunks=n_chunks),
        grid=(b, nt),
        in_specs=[_const_spec(dec.shape), fwd, fwd, bwd, bwd],
        out_specs=[pl.BlockSpec(state_shape, lambda i, t: (i, t, 0, 0, 0)),
                   pl.BlockSpec(state_shape, lambda i, t: (i, nt - 1 - t, 0, 0, 0))],
        out_shape=[out, out],
        scratch_shapes=[pltpu.VMEM((N_PAIRS, LANES, LANES), F32), pltpu.VMEM((N_PAIRS, LANES, LANES), F32)],
        compiler_params=_params("arbitrary", "arbitrary"),
        name="ret_state",
    )(dec, kf, v, kb, v)


def _mixer_kernel(dec_ref, x_ref, uprev_ref, u_ref, unext_ref, q_ref, qf_ref, qb_ref, k_ref, v_ref, sg_ref, gate_ref,
                  sf_ref, sb_ref, dww_ref, dwb_ref, lng_ref, lnb_ref, wpw_ref, gng_ref, wro_ref,
                  wmix_ref, g3_ref, pavg_ref, o_ref, upad_ref, ushift_ref, cact_ref, *, ts):
    t = pl.program_id(1)
    last = pl.num_programs(1) - 1
    n_chunks = ts // RET_CHUNK

    upad_ref[0:CONV_HALO, :] = jnp.where(t > 0, uprev_ref[0].astype(F32), 0.0)
    upad_ref[CONV_HALO:CONV_HALO + ts, :] = u_ref[0].astype(F32)
    upad_ref[CONV_HALO + ts:, :] = jnp.where(t < last, unext_ref[0].astype(F32), 0.0)
    first_tap = CONV_HALO - CONV_WIDTH // 2
    n_shift = ts + 2 * CONV_HALO - SUBLANES
    for r in range(1, SUBLANES):
        ushift_ref[r - 1] = upad_ref[r:r + n_shift, :]
    row_groups = CONV_ROWS // SUBLANES

    def conv_rows(r0, after):
        acc = jnp.concatenate([dwb_ref[...]] * row_groups, axis=0)
        if after is not None:
            bits = pltpu.bitcast(after[0:SUBLANES, 0:LANES], jnp.uint32)
            zero = pltpu.bitcast(lax.shift_right_logical(lax.shift_right_logical(bits, jnp.uint32(31)),
                                                         jnp.uint32(1)), F32)
            acc = acc + jnp.concatenate([jnp.concatenate([zero] * (CONV_DIM // LANES), axis=1)] * row_groups,
                                        axis=0)
        for w in range(CONV_WIDTH):
            r = (first_tap + w) % SUBLANES
            base = r0 + first_tap + w - r
            taps = (upad_ref[base:base + CONV_ROWS, :] if r == 0
                    else ushift_ref[r - 1, base:base + CONV_ROWS, :])
            acc = acc + taps * jnp.concatenate([dww_ref[w]] * row_groups, axis=0)
        mu = jnp.mean(acc, axis=-1, keepdims=True)
        d = acc - mu
        var = jnp.mean(d * d, axis=-1, keepdims=True)
        z = (d * lax.rsqrt(var + NORM_EPS) * jnp.concatenate([lng_ref[...]] * row_groups, axis=0)
             + jnp.concatenate([lnb_ref[...]] * row_groups, axis=0))
        cact_ref[r0:r0 + CONV_ROWS, :] = (z * jax.nn.sigmoid(z)).astype(BF16)

    ri = lax.broadcasted_iota(jnp.int32, (RET_CHUNK, RET_CHUNK), 0)
    ci = lax.broadcasted_iota(jnp.int32, (RET_CHUNK, RET_CHUNK), 1)
    ahead = jnp.maximum(ri - ci, 0).astype(F32)
    behind = jnp.maximum(ci - ri, 0).astype(F32)
    causal = ri >= ci
    head_lanes = (ci < RET_HEAD_DIM, ci >= RET_HEAD_DIM)
    decay = []
    for h in range(RET_HEADS):
        lf = -jnp.exp(dec_ref[2 * N_PAIRS + h:2 * N_PAIRS + h + 1, :])
        lb = -jnp.exp(dec_ref[2 * N_PAIRS + RET_HEADS + h:2 * N_PAIRS + RET_HEADS + h + 1, :])
        decay.append(jnp.where(causal, jnp.exp(lf * ahead), jnp.exp(lb * behind)))

    def retention_unit(n, p):
        rows = slice(n * RET_CHUNK, (n + 1) * RET_CHUNK)
        cols = slice(p * LANES, (p + 1) * LANES)
        qp = q_ref[0, rows, cols]
        vp = v_ref[0, rows, cols]
        zero = jnp.zeros_like(qp)
        scores = _dot_nt(jnp.concatenate([jnp.where(m, qp, zero) for m in head_lanes], axis=0),
                         k_ref[0, rows, cols])
        lhs = [(scores[sub * RET_CHUNK:(sub + 1) * RET_CHUNK] * decay[2 * p + sub]).astype(BF16)
               for sub in range(2)]
        lhs += [qf_ref[0, rows, cols], qb_ref[0, rows, cols]]
        rhs = [jnp.where(m, vp, zero) for m in head_lanes] + [sf_ref[0, n, p], sb_ref[0, n, p]]
        return _dot(jnp.concatenate(lhs, axis=1), jnp.concatenate(rhs, axis=0))

    def project(rows, y):
        conv_out = _dot(cact_ref[rows, :], wpw_ref[...])
        d = y - _dot(y.astype(BF16), pavg_ref[...])
        var = _dot((d * d).astype(BF16), pavg_ref[...])
        yn = d * lax.rsqrt(var + NORM_EPS) * gng_ref[...]
        ret_out = _dot((sg_ref[0, rows, :].astype(F32) * yn).astype(BF16), wro_ref[...])
        mixed = (gate_ref[0, rows, :D_MODEL].astype(F32) * conv_out
                 + gate_ref[0, rows, D_MODEL:].astype(F32) * ret_out)
        m = _dot(mixed.astype(BF16), wmix_ref[...])
        o_ref[0, rows, :] = x_ref[0, rows, :] + _rms(m, g3_ref[...])

    units = [(n, p) for n in range(n_chunks) for p in range(N_PAIRS)]
    conv_blocks = list(range(0, ts, CONV_ROWS))
    per_unit = -(-len(conv_blocks) // len(units))
    y = {}
    for i, (n, p) in enumerate(units):
        y[n, p] = retention_unit(n, p)
        for r0 in conv_blocks[i * per_unit:(i + 1) * per_unit]:
            conv_rows(r0, y[units[i - 1]] if i >= 1 else None)
    y = jnp.concatenate([jnp.concatenate([y[n, p] for p in range(N_PAIRS)], axis=1) for n in range(n_chunks)],
                        axis=0)
    project(slice(0, ts), y)


def _mixer(dec, x, u, q, qf, qb, k, v, sg, gates, sf, sb, dw_w, dw_b, ln_g, ln_b, w_pw, gn_g, w_ro, w_mix, g3,
           pavg, layer):
    b, s, _ = x.shape
    ts = SEQ_TILE
    nt = s // ts
    n_chunks = ts // RET_CHUNK
    halo_per_tile = ts // CONV_HALO
    n_halo = s // CONV_HALO

    def seq(width):
        return pl.BlockSpec((1, ts, width), lambda i, t: (i, t, 0))

    halo_prev = pl.BlockSpec((1, CONV_HALO, CONV_DIM),
                             lambda i, t: (i, jnp.maximum(t * halo_per_tile - 1, 0), 0))
    halo_next = pl.BlockSpec((1, CONV_HALO, CONV_DIM),
                             lambda i, t: (i, jnp.minimum((t + 1) * halo_per_tile, n_halo - 1), 0))
    state = pl.BlockSpec((1, n_chunks, N_PAIRS, LANES, LANES), lambda i, t: (i, t, 0, 0, 0))
    ret = seq(RET_DIM)
    return pl.pallas_call(
        functools.partial(_mixer_kernel, ts=ts),
        grid=(b, nt),
        in_specs=[_const_spec(dec.shape), seq(D_MODEL), halo_prev, seq(CONV_DIM), halo_next,
                  ret, ret, ret, ret, ret, ret, seq(2 * D_MODEL), state, state,
                  _const_spec((CONV_WIDTH, SUBLANES, CONV_DIM)), _const_spec((SUBLANES, CONV_DIM)),
                  _const_spec((SUBLANES, CONV_DIM)), _const_spec((SUBLANES, CONV_DIM)),
                  _layer_spec((CONV_DIM, D_MODEL), layer), _const_spec((1, RET_DIM)),
                  _layer_spec((RET_DIM, D_MODEL), layer), _layer_spec((D_MODEL, D_MODEL), layer),
                  _const_spec((1, D_MODEL)), _const_spec((RET_DIM, RET_DIM))],
        out_specs=seq(D_MODEL),
        out_shape=jax.ShapeDtypeStruct((b, s, D_MODEL), F32),
        scratch_shapes=[pltpu.VMEM((ts + 2 * CONV_HALO, CONV_DIM), F32),
                        pltpu.VMEM((SUBLANES - 1, ts + 2 * CONV_HALO - SUBLANES, CONV_DIM), F32),
                        pltpu.VMEM((ts, CONV_DIM), BF16)],
        compiler_params=_params("parallel", "parallel"),
        name="mixer",
    )(dec, x, u, u, u, q, qf, qb, k, v, sg, gates, sf, sb, dw_w, dw_b, ln_g, ln_b, w_pw, gn_g, w_ro, w_mix, g3,
      pavg)


def _mem_kv_kernel(m_ref, g_ref, w_ref, o_ref):
    o_ref[...] = _dot(_rms(m_ref[...], g_ref[...]).astype(BF16), w_ref[...]).astype(BF16)


def _mem_kv(mem, g, w_kv, layer):
    t = mem.shape[0]
    tile = min(t, TOKEN_TILE)
    return pl.pallas_call(
        _mem_kv_kernel,
        grid=(t // tile,),
        in_specs=[pl.BlockSpec((tile, D_MODEL), lambda i: (i, 0)), _const_spec((1, D_MODEL)),
                  _layer_spec((D_MODEL, 2 * D_MODEL), layer)],
        out_specs=pl.BlockSpec((tile, 2 * D_MODEL), lambda i: (i, 0)),
        out_shape=jax.ShapeDtypeStruct((t, 2 * D_MODEL), BF16),
        compiler_params=_params("parallel"),
        name="mem_kv",
    )(mem, g, w_kv)


def _xattn_kernel(x_ref, gin_ref, gout_ref, wq_ref, k_ref, v_ref, wo_ref, o_ref, att_ref):
    x = x_ref[0]
    q = _dot(_rms(x, gin_ref[...]).astype(BF16), wq_ref[...])
    for h in range(XATTN_HEADS):
        cols = slice(h * XATTN_HEAD_DIM, (h + 1) * XATTN_HEAD_DIM)
        logits = _dot_nt(q[:, cols].astype(BF16), k_ref[0, :, cols]) * (XATTN_HEAD_DIM ** -0.5)
        e = jnp.exp(logits - jnp.max(logits, axis=-1, keepdims=True))
        p = e / jnp.sum(e, axis=-1, keepdims=True)
        att_ref[:, cols] = _dot(p.astype(BF16), v_ref[0, :, cols]).astype(BF16)
    a = _dot(att_ref[...], wo_ref[...])
    o_ref[0] = x + _rms(a, gout_ref[...])


def _xattn(x, kv, g_in, g_out, w_q, w_o, layer):
    b, s, _ = x.shape
    n_mem = kv.shape[1]
    ts = XATTN_TILE
    seq = pl.BlockSpec((1, ts, D_MODEL), lambda i, t: (i, t, 0))
    return pl.pallas_call(
        _xattn_kernel,
        grid=(b, s // ts),
        in_specs=[seq, _const_spec((1, D_MODEL)), _const_spec((1, D_MODEL)),
                  _layer_spec((D_MODEL, D_MODEL), layer),
                  pl.BlockSpec((1, n_mem, D_MODEL), lambda i, t: (i, 0, 0)),
                  pl.BlockSpec((1, n_mem, D_MODEL), lambda i, t: (i, 0, 1)),
                  _layer_spec((D_MODEL, D_MODEL), layer)],
        out_specs=seq,
        out_shape=jax.ShapeDtypeStruct((b, s, D_MODEL), F32),
        scratch_shapes=[pltpu.VMEM((ts, D_MODEL), BF16)],
        compiler_params=_params("parallel", "parallel"),
        name="xattn",
    )(x, g_in, g_out, w_q, kv, kv, w_o)


def _rope_tables(s):
    half = RET_HEAD_DIM // 2
    inv_freq = ROPE_BASE ** (-jnp.arange(half, dtype=F32) / half)
    ang = jnp.arange(s, dtype=F32)[:, None] * inv_freq[None, :]
    cos, sin = jnp.cos(ang), jnp.sin(ang)
    return (jnp.tile(cos, (1, LANES // half)),
            jnp.tile(jnp.concatenate([-sin, sin], axis=1), (1, LANES // RET_HEAD_DIM)))


def _decay_rows(decay_fwd, decay_bwd):
    pair = lambda d: jnp.repeat(d, RET_HEAD_DIM).reshape(N_PAIRS, LANES)
    head = lambda d: jnp.broadcast_to(d[:, None], (RET_HEADS, LANES))
    return jnp.concatenate([pair(decay_fwd), pair(decay_bwd), head(decay_fwd), head(decay_bwd)], axis=0)


def _trunk(x, mem, w, small, cos, sin, pavg):
    b, s, _ = x.shape
    t = b * s
    n_mem = mem.shape[1]
    row = lambda a: a.reshape(1, -1)
    rows8 = lambda a: jnp.broadcast_to(a[..., None, :], a.shape[:-1] + (SUBLANES, a.shape[-1]))
    flat = lambda a: a.reshape(t, a.shape[-1])
    seq = lambda a: a.reshape(b, s, a.shape[-1])
    mem2 = mem.reshape(b * n_mem, D_MODEL)
    x = flat(x)
    for l, p in enumerate(small):
        ng = p["norm_g"]
        x = _ffn(x, row(ng[0]), row(ng[1]), w["ffn1_w_gu"], w["ffn1_w_down"], l)
        proj = _mixer_in(p["dec"], x, row(ng[2]), w["w_in"], row(p["gate_b"]), cos, sin, s, l)
        u, q, qf, qb, kf, kb, k, v, sg, gates = map(seq, proj)
        sf, sb = _ret_state(p["dec"], kf, kb, v)
        x = _mixer(p["dec"], seq(x), u, q, qf, qb, k, v, sg, gates, sf, sb, rows8(p["conv_dw_w"]),
                   rows8(p["conv_dw_b"]), rows8(p["conv_ln_g"]), rows8(p["conv_ln_b"]), w["conv_w_pw"],
                   row(p["ret_gn_g"]), w["ret_w_out"], w["w_mix_out"], row(ng[3]), pavg, l)
        kv = _mem_kv(mem2, row(p["mem_norm_g"]), w["xattn_w_kv"], l).reshape(b, n_mem, 2 * D_MODEL)
        x = _xattn(x, kv, row(ng[4]), row(ng[5]), w["xattn_w_q"], w["xattn_w_o"], l)
        x = _ffn(flat(x), row(ng[6]), row(ng[7]), w["ffn2_w_gu"], w["ffn2_w_down"], l)
    return seq(x)


def kernel(x_prompt, x_sample, mem_prompt, mem_sample, norm_g, ffn1_w_gu, ffn1_w_down, w_in, conv_dw_w, conv_dw_b,
           conv_ln_g, conv_ln_b, conv_w_pw, ret_decay_fwd, ret_decay_bwd, ret_gn_g, ret_w_out, gate_b, w_mix_out,
           mem_norm_g, xattn_w_q, xattn_w_kv, xattn_w_o, ffn2_w_gu, ffn2_w_down):
    w = {name: a.astype(BF16) for name, a in dict(
        ffn1_w_gu=ffn1_w_gu, ffn1_w_down=ffn1_w_down, w_in=w_in, conv_w_pw=conv_w_pw, ret_w_out=ret_w_out,
        w_mix_out=w_mix_out, xattn_w_q=xattn_w_q, xattn_w_kv=xattn_w_kv, xattn_w_o=xattn_w_o,
        ffn2_w_gu=ffn2_w_gu, ffn2_w_down=ffn2_w_down).items()}
    small = [dict(norm_g=norm_g[l], conv_dw_w=conv_dw_w[l], conv_dw_b=conv_dw_b[l], conv_ln_g=conv_ln_g[l],
                  conv_ln_b=conv_ln_b[l], dec=_decay_rows(ret_decay_fwd[l], ret_decay_bwd[l]), ret_gn_g=ret_gn_g[l],
                  gate_b=gate_b[l], mem_norm_g=mem_norm_g[l]) for l in range(norm_g.shape[0])]
    cos, sin = _rope_tables(max(x_prompt.shape[1], x_sample.shape[1]))
    head_of = jnp.arange(RET_DIM) // RET_HEAD_DIM
    pavg = jnp.where(head_of[:, None] == head_of[None, :], 1.0 / RET_HEAD_DIM, 0.0).astype(BF16)
    return (_trunk(x_prompt, mem_prompt, w, small, cos, sin, pavg),
            _trunk(x_sample, mem_sample, w, small, cos, sin, pavg))
```

```python
import functools

import jax
import jax.numpy as jnp
from jax import lax
from jax.experimental import pallas as pl
from jax.experimental.pallas import tpu as pltpu

D_MODEL = 1024
D_FF = 2816
CONV_DIM = 512
CONV_WIDTH = 31
CONV_HALO = 16
RET_HEADS = 8
RET_HEAD_DIM = 64
RET_DIM = RET_HEADS * RET_HEAD_DIM
RET_CHUNK = 128
LANES = 128
SUBLANES = 8
N_PAIRS = RET_DIM // LANES
XATTN_HEADS = 4
XATTN_HEAD_DIM = D_MODEL // XATTN_HEADS
ROPE_BASE = 10000.0
NORM_EPS = 1e-6
IN_COLS = 2 * CONV_DIM + 4 * RET_DIM + 2 * D_MODEL

TOKEN_TILE = 1024
SEQ_TILE = 1024
FFN_CHUNK = 1024
XATTN_TILE = 2048
CONV_COLS = 128
CONV_ROWS = 32
VMEM_LIMIT_BYTES = 58 * 1024 * 1024

F32 = jnp.float32
BF16 = jnp.bfloat16


def _const_spec(shape):
    nd = len(shape)
    return pl.BlockSpec(shape, lambda *_: (0,) * nd, pipeline_mode=pl.Buffered(1))


def _layer_spec(shape, layer):
    nd = len(shape)
    return pl.BlockSpec((None,) + tuple(shape), lambda *_: (layer,) + (0,) * nd, pipeline_mode=pl.Buffered(1))


def _params(*semantics):
    return pltpu.CompilerParams(dimension_semantics=semantics, vmem_limit_bytes=VMEM_LIMIT_BYTES)


def _rms(x, g):
    return x * lax.rsqrt(jnp.mean(x * x, axis=-1, keepdims=True) + NORM_EPS) * g


def _dot(a, b):
    return jnp.dot(a, b, preferred_element_type=F32)


def _dot_nt(a, b):
    return lax.dot_general(a, b, (((1,), (1,)), ((), ())), preferred_element_type=F32)


def _dot_tn(a, b):
    return lax.dot_general(a, b, (((0,), (0,)), ((), ())), preferred_element_type=F32)


def _pair_log_gamma(dec_ref, p):
    return -jnp.exp(dec_ref[p:p + 1, :]), -jnp.exp(dec_ref[N_PAIRS + p:N_PAIRS + p + 1, :])


def _ffn_kernel(x_ref, gin_ref, gout_ref, wgu_ref, wd_ref, o_ref):
    x = x_ref[...]
    h = _rms(x, gin_ref[...]).astype(BF16)
    f = None
    for a in range(0, D_FF, FFN_CHUNK):
        b = min(a + FFN_CHUNK, D_FF)
        gate = _dot(h, wgu_ref[:, a:b])
        up = _dot(h, wgu_ref[:, D_FF + a:D_FF + b])
        part = _dot((gate * jax.nn.sigmoid(gate) * up).astype(BF16), wd_ref[a:b, :])
        f = part if f is None else f + part
    o_ref[...] = x + 0.5 * _rms(f, gout_ref[...])


def _ffn(x, g_in, g_out, w_gu, w_down, layer):
    t = x.shape[0]
    tile = pl.BlockSpec((TOKEN_TILE, D_MODEL), lambda i: (i, 0))
    return pl.pallas_call(
        _ffn_kernel,
        grid=(t // TOKEN_TILE,),
        in_specs=[tile, _const_spec((1, D_MODEL)), _const_spec((1, D_MODEL)),
                  _layer_spec((D_MODEL, 2 * D_FF), layer), _layer_spec((D_FF, D_MODEL), layer)],
        out_specs=tile,
        out_shape=jax.ShapeDtypeStruct((t, D_MODEL), F32),
        compiler_params=_params("parallel"),
        name="ffn",
    )(x, g_in, g_out, w_gu, w_down)


def _mixer_in_kernel(dec_ref, x_ref, g_ref, w_ref, gb_ref, cos_ref, sin_ref,
                     u_ref, q_ref, qf_ref, qb_ref, kf_ref, kb_ref, k_ref, v_ref, sg_ref, gate_ref):
    h = _rms(x_ref[...], g_ref[...]).astype(BF16)

    def piece(j):
        return _dot(h, w_ref[:, j * RET_DIM:(j + 1) * RET_DIM])

    ci = piece(0), piece(1)
    u_ref[...] = (ci[0] * jax.nn.sigmoid(ci[1])).astype(BF16)

    cos = cos_ref[...]
    sin = sin_ref[...]
    lane = lax.broadcasted_iota(jnp.int32, cos.shape, 1)
    first_half = (lane % RET_HEAD_DIM) < (RET_HEAD_DIM // 2)
    row = lax.broadcasted_iota(jnp.int32, (RET_CHUNK, LANES), 0).astype(F32)
    c = float(RET_CHUNK)

    def rotary(xp, scale, p):
        xc = xp[:, p * LANES:(p + 1) * LANES]
        partner = jnp.where(first_half, pltpu.roll(xc, LANES - RET_HEAD_DIM // 2, 1),
                            pltpu.roll(xc, RET_HEAD_DIM // 2, 1))
        return (xc * cos + partner * sin) * scale

    def store_decayed(x, decay, out_ref, cols):
        for n in range(x.shape[0] // RET_CHUNK):
            rows = slice(n * RET_CHUNK, (n + 1) * RET_CHUNK)
            out_ref[rows, cols] = (x[rows] * decay).astype(BF16)

    qp = piece(2)
    for p in range(N_PAIRS):
        cols = slice(p * LANES, (p + 1) * LANES)
        lgf, lgb = _pair_log_gamma(dec_ref, p)
        q = rotary(qp, 1.0, p)
        q_ref[:, cols] = q.astype(BF16)
        store_decayed(q, jnp.exp(lgf * (row + 1.0)), qf_ref, cols)
        store_decayed(q, jnp.exp(lgb * (c - row)), qb_ref, cols)
    kp = piece(3)
    for p in range(N_PAIRS):
        cols = slice(p * LANES, (p + 1) * LANES)
        lgf, lgb = _pair_log_gamma(dec_ref, p)
        k = rotary(kp, RET_HEAD_DIM ** -0.5, p)
        k_ref[:, cols] = k.astype(BF16)
        store_decayed(k, jnp.exp(lgf * (c - 1.0 - row)), kf_ref, cols)
        store_decayed(k, jnp.exp(lgb * row), kb_ref, cols)
    v_ref[...] = piece(4).astype(BF16)
    g = piece(5)
    sg_ref[...] = (g * jax.nn.sigmoid(g)).astype(BF16)
    for j in range(6, IN_COLS // RET_DIM):
        cols = slice((j - 6) * RET_DIM, (j - 5) * RET_DIM)
        gate_ref[:, cols] = jax.nn.sigmoid(piece(j) + gb_ref[:, cols]).astype(BF16)


def _mixer_in(dec, x, g, w_in, gate_b, cos, sin, seq_len, layer):
    t = x.shape[0]
    tiles_per_seq = seq_len // TOKEN_TILE

    def rows(width):
        return pl.BlockSpec((TOKEN_TILE, width), lambda i: (i, 0))

    rope = pl.BlockSpec((TOKEN_TILE, LANES), lambda i: (i % tiles_per_seq, 0))
    out_widths = (CONV_DIM,) + (RET_DIM,) * 8 + (2 * D_MODEL,)
    return pl.pallas_call(
        _mixer_in_kernel,
        grid=(t // TOKEN_TILE,),
        in_specs=[_const_spec(dec.shape), rows(D_MODEL), _const_spec((1, D_MODEL)),
                  _layer_spec((D_MODEL, IN_COLS), layer), _const_spec((1, 2 * D_MODEL)), rope, rope],
        out_specs=[rows(w) for w in out_widths],
        out_shape=[jax.ShapeDtypeStruct((t, w), BF16) for w in out_widths],
        compiler_params=_params("parallel"),
        name="mixer_in",
    )(dec, x, g, w_in, gate_b, cos, sin)


def _ret_state_kernel(dec_ref, kf_ref, vf_ref, kb_ref, vb_ref, sf_out, sb_out, sf_ref, sb_ref, *, n_chunks):
    @pl.when(pl.program_id(1) == 0)
    def _():
        sf_ref[...] = jnp.zeros_like(sf_ref)
        sb_ref[...] = jnp.zeros_like(sb_ref)

    ri = lax.broadcasted_iota(jnp.int32, (RET_CHUNK, LANES), 0)
    ci = lax.broadcasted_iota(jnp.int32, (RET_CHUNK, LANES), 1)
    same_head = (ri // RET_HEAD_DIM) == (ci // RET_HEAD_DIM)
    c = float(RET_CHUNK)

    def chunk_kv(k_ref, v_ref, n, p):
        rows = slice(n * RET_CHUNK, (n + 1) * RET_CHUNK)
        cols = slice(p * LANES, (p + 1) * LANES)
        return jnp.where(same_head, _dot_tn(k_ref[0, rows, cols], v_ref[0, rows, cols]), 0.0)

    for p in range(N_PAIRS):
        lgf, lgb = _pair_log_gamma(dec_ref, p)
        step_f = jnp.exp(lgf * c)
        step_b = jnp.exp(lgb * c)
        s = sf_ref[p]
        for n in range(n_chunks):
            sf_out[0, n, p] = s.astype(BF16)
            s = s * step_f + chunk_kv(kf_ref, vf_ref, n, p)
        sf_ref[p] = s
        s = sb_ref[p]
        for n in reversed(range(n_chunks)):
            sb_out[0, n, p] = s.astype(BF16)
            s = s * step_b + chunk_kv(kb_ref, vb_ref, n, p)
        sb_ref[p] = s


def _ret_state(dec, kf, kb, v):
    b, s, _ = v.shape
    nt = s // SEQ_TILE
    n_chunks = SEQ_TILE // RET_CHUNK
    fwd = pl.BlockSpec((1, SEQ_TILE, RET_DIM), lambda i, t: (i, t, 0))
    bwd = pl.BlockSpec((1, SEQ_TILE, RET_DIM), lambda i, t: (i, nt - 1 - t, 0))
    state_shape = (1, n_chunks, N_PAIRS, LANES, LANES)
    out = jax.ShapeDtypeStruct((b, s // RET_CHUNK, N_PAIRS, LANES, LANES), BF16)
    return pl.pallas_call(
        functools.partial(_ret_state_kernel, n_chunks=n_chunks),
        grid=(b, nt),
        in_specs=[_const_spec(dec.shape), fwd, fwd, bwd, bwd],
        out_specs=[pl.BlockSpec(state_shape, lambda i, t: (i, t, 0, 0, 0)),
                   pl.BlockSpec(state_shape, lambda i, t: (i, nt - 1 - t, 0, 0, 0))],
        out_shape=[out, out],
        scratch_shapes=[pltpu.VMEM((N_PAIRS, LANES, LANES), F32), pltpu.VMEM((N_PAIRS, LANES, LANES), F32)],
        compiler_params=_params("arbitrary", "arbitrary"),
        name="ret_state",
    )(dec, kf, v, kb, v)


def _mixer_kernel(dec_ref, x_ref, uprev_ref, u_ref, unext_ref, q_ref, qf_ref, qb_ref, k_ref, v_ref, sg_ref, gate_ref,
                  sf_ref, sb_ref, dww_ref, dwb_ref, lng_ref, lnb_ref, wpw_ref, gng_ref, wro_ref,
                  wmix_ref, g3_ref, pavg_ref, o_ref, upad_ref, ushift_ref, cact_ref, *, ts):
    t = pl.program_id(1)
    last = pl.num_programs(1) - 1
    n_chunks = ts // RET_CHUNK

    upad_ref[0:CONV_HALO, :] = jnp.where(t > 0, uprev_ref[0].astype(F32), 0.0)
    upad_ref[CONV_HALO:CONV_HALO + ts, :] = u_ref[0].astype(F32)
    upad_ref[CONV_HALO + ts:, :] = jnp.where(t < last, unext_ref[0].astype(F32), 0.0)
    first_tap = CONV_HALO - CONV_WIDTH // 2
    n_shift = ts + 2 * CONV_HALO - SUBLANES
    for r in range(1, SUBLANES):
        ushift_ref[r - 1] = upad_ref[r:r + n_shift, :]
    row_groups = CONV_ROWS // SUBLANES

    def conv_rows(r0, after):
        acc = jnp.concatenate([dwb_ref[...]] * row_groups, axis=0)
        if after is not None:
            bits = pltpu.bitcast(after[0:SUBLANES, 0:LANES], jnp.uint32)
            zero = pltpu.bitcast(lax.shift_right_logical(lax.shift_right_logical(bits, jnp.uint32(31)),
                                                         jnp.uint32(1)), F32)
            acc = acc + jnp.concatenate([jnp.concatenate([zero] * (CONV_DIM // LANES), axis=1)] * row_groups,
                                        axis=0)
        halves = []
        for c0 in range(0, CONV_DIM, CONV_COLS):
            ch = slice(c0, c0 + CONV_COLS)
            part = acc[:, ch]
            for w in range(CONV_WIDTH):
                r = (first_tap + w) % SUBLANES
                base = r0 + first_tap + w - r
                taps = (upad_ref[base:base + CONV_ROWS, ch] if r == 0
                        else ushift_ref[r - 1, base:base + CONV_ROWS, ch])
                part = part + taps * jnp.concatenate([dww_ref[w, :, ch]] * row_groups, axis=0)
            halves.append(part)
        acc = jnp.concatenate(halves, axis=1)
        mu = jnp.mean(acc, axis=-1, keepdims=True)
        d = acc - mu
        var = jnp.mean(d * d, axis=-1, keepdims=True)
        z = (d * lax.rsqrt(var + NORM_EPS) * jnp.concatenate([lng_ref[...]] * row_groups, axis=0)
             + jnp.concatenate([lnb_ref[...]] * row_groups, axis=0))
        cact_ref[r0:r0 + CONV_ROWS, :] = (z * jax.nn.sigmoid(z)).astype(BF16)

    ri = lax.broadcasted_iota(jnp.int32, (RET_CHUNK, RET_CHUNK), 0)
    ci = lax.broadcasted_iota(jnp.int32, (RET_CHUNK, RET_CHUNK), 1)
    ahead = jnp.maximum(ri - ci, 0).astype(F32)
    behind = jnp.maximum(ci - ri, 0).astype(F32)
    causal = ri >= ci
    head_lanes = (ci < RET_HEAD_DIM, ci >= RET_HEAD_DIM)
    decay = []
    for h in range(RET_HEADS):
        lf = -jnp.exp(dec_ref[2 * N_PAIRS + h:2 * N_PAIRS + h + 1, :])
        lb = -jnp.exp(dec_ref[2 * N_PAIRS + RET_HEADS + h:2 * N_PAIRS + RET_HEADS + h + 1, :])
        decay.append(jnp.where(causal, jnp.exp(lf * ahead), jnp.exp(lb * behind)))

    def retention_unit(n, p):
        rows = slice(n * RET_CHUNK, (n + 1) * RET_CHUNK)
        cols = slice(p * LANES, (p + 1) * LANES)
        qp = q_ref[0, rows, cols]
        vp = v_ref[0, rows, cols]
        zero = jnp.zeros_like(qp)
        scores = _dot_nt(jnp.concatenate([jnp.where(m, qp, zero) for m in head_lanes], axis=0),
                         k_ref[0, rows, cols])
        lhs = [(scores[sub * RET_CHUNK:(sub + 1) * RET_CHUNK] * decay[2 * p + sub]).astype(BF16)
               for sub in range(2)]
        lhs += [qf_ref[0, rows, cols], qb_ref[0, rows, cols]]
        rhs = [jnp.where(m, vp, zero) for m in head_lanes] + [sf_ref[0, n, p], sb_ref[0, n, p]]
        return _dot(jnp.concatenate(lhs, axis=1), jnp.concatenate(rhs, axis=0))

    def project(rows, y):
        conv_out = _dot(cact_ref[rows, :], wpw_ref[...])
        d = y - _dot(y.astype(BF16), pavg_ref[...])
        var = _dot((d * d).astype(BF16), pavg_ref[...])
        yn = d * lax.rsqrt(var + NORM_EPS) * gng_ref[...]
        ret_out = _dot((sg_ref[0, rows, :].astype(F32) * yn).astype(BF16), wro_ref[...])
        mixed = (gate_ref[0, rows, :D_MODEL].astype(F32) * conv_out
                 + gate_ref[0, rows, D_MODEL:].astype(F32) * ret_out)
        m = _dot(mixed.astype(BF16), wmix_ref[...])
        o_ref[0, rows, :] = x_ref[0, rows, :] + _rms(m, g3_ref[...])

    units = [(n, p) for n in range(n_chunks) for p in range(N_PAIRS)]
    conv_blocks = list(range(0, ts, CONV_ROWS))
    per_unit = -(-len(conv_blocks) // len(units))
    y = {}
    for i, (n, p) in enumerate(units):
        y[n, p] = retention_unit(n, p)
        for r0 in conv_blocks[i * per_unit:(i + 1) * per_unit]:
            conv_rows(r0, y[units[i - 1]] if i >= 1 else None)
    y = jnp.concatenate([jnp.concatenate([y[n, p] for p in range(N_PAIRS)], axis=1) for n in range(n_chunks)],
                        axis=0)
    project(slice(0, ts), y)


def _mixer(dec, x, u, q, qf, qb, k, v, sg, gates, sf, sb, dw_w, dw_b, ln_g, ln_b, w_pw, gn_g, w_ro, w_mix, g3,
           pavg, layer):
    b, s, _ = x.shape
    ts = SEQ_TILE
    nt = s // ts
    n_chunks = ts // RET_CHUNK
    halo_per_tile = ts // CONV_HALO
    n_halo = s // CONV_HALO

    def seq(width):
        return pl.BlockSpec((1, ts, width), lambda i, t: (i, t, 0))

    halo_prev = pl.BlockSpec((1, CONV_HALO, CONV_DIM),
                             lambda i, t: (i, jnp.maximum(t * halo_per_tile - 1, 0), 0))
    halo_next = pl.BlockSpec((1, CONV_HALO, CONV_DIM),
                             lambda i, t: (i, jnp.minimum((t + 1) * halo_per_tile, n_halo - 1), 0))
    state = pl.BlockSpec((1, n_chunks, N_PAIRS, LANES, LANES), lambda i, t: (i, t, 0, 0, 0))
    ret = seq(RET_DIM)
    return pl.pallas_call(
        functools.partial(_mixer_kernel, ts=ts),
        grid=(b, nt),
        in_specs=[_const_spec(dec.shape), seq(D_MODEL), halo_prev, seq(CONV_DIM), halo_next,
                  ret, ret, ret, ret, ret, ret, seq(2 * D_MODEL), state, state,
                  _const_spec((CONV_WIDTH, SUBLANES, CONV_DIM)), _const_spec((SUBLANES, CONV_DIM)),
                  _const_spec((SUBLANES, CONV_DIM)), _const_spec((SUBLANES, CONV_DIM)),
                  _layer_spec((CONV_DIM, D_MODEL), layer), _const_spec((1, RET_DIM)),
                  _layer_spec((RET_DIM, D_MODEL), layer), _layer_spec((D_MODEL, D_MODEL), layer),
                  _const_spec((1, D_MODEL)), _const_spec((RET_DIM, RET_DIM))],
        out_specs=seq(D_MODEL),
        out_shape=jax.ShapeDtypeStruct((b, s, D_MODEL), F32),
        scratch_shapes=[pltpu.VMEM((ts + 2 * CONV_HALO, CONV_DIM), F32),
                        pltpu.VMEM((SUBLANES - 1, ts + 2 * CONV_HALO - SUBLANES, CONV_DIM), F32),
                        pltpu.VMEM((ts, CONV_DIM), BF16)],
        compiler_params=_params("parallel", "parallel"),
        name="mixer",
    )(dec, x, u, u, u, q, qf, qb, k, v, sg, gates, sf, sb, dw_w, dw_b, ln_g, ln_b, w_pw, gn_g, w_ro, w_mix, g3,
      pavg)


def _mem_kv_kernel(m_ref, g_ref, w_ref, o_ref):
    o_ref[...] = _dot(_rms(m_ref[...], g_ref[...]).astype(BF16), w_ref[...]).astype(BF16)


def _mem_kv(mem, g, w_kv, layer):
    t = mem.shape[0]
    tile = min(t, TOKEN_TILE)
    return pl.pallas_call(
        _mem_kv_kernel,
        grid=(t // tile,),
        in_specs=[pl.BlockSpec((tile, D_MODEL), lambda i: (i, 0)), _const_spec((1, D_MODEL)),
                  _layer_spec((D_MODEL, 2 * D_MODEL), layer)],
        out_specs=pl.BlockSpec((tile, 2 * D_MODEL), lambda i: (i, 0)),
        out_shape=jax.ShapeDtypeStruct((t, 2 * D_MODEL), BF16),
        compiler_params=_params("parallel"),
        name="mem_kv",
    )(mem, g, w_kv)


def _xattn_kernel(x_ref, gin_ref, gout_ref, wq_ref, k_ref, v_ref, wo_ref, o_ref, att_ref):
    x = x_ref[0]
    q = _dot(_rms(x, gin_ref[...]).astype(BF16), wq_ref[...])
    for h in range(XATTN_HEADS):
        cols = slice(h * XATTN_HEAD_DIM, (h + 1) * XATTN_HEAD_DIM)
        logits = _dot_nt(q[:, cols].astype(BF16), k_ref[0, :, cols]) * (XATTN_HEAD_DIM ** -0.5)
        e = jnp.exp(logits - jnp.max(logits, axis=-1, keepdims=True))
        p = e / jnp.sum(e, axis=-1, keepdims=True)
        att_ref[:, cols] = _dot(p.astype(BF16), v_ref[0, :, cols]).astype(BF16)
    a = _dot(att_ref[...], wo_ref[...])
    o_ref[0] = x + _rms(a, gout_ref[...])


def _xattn(x, kv, g_in, g_out, w_q, w_o, layer):
    b, s, _ = x.shape
    n_mem = kv.shape[1]
    ts = XATTN_TILE
    seq = pl.BlockSpec((1, ts, D_MODEL), lambda i, t: (i, t, 0))
    return pl.pallas_call(
        _xattn_kernel,
        grid=(b, s // ts),
        in_specs=[seq, _const_spec((1, D_MODEL)), _const_spec((1, D_MODEL)),
                  _layer_spec((D_MODEL, D_MODEL), layer),
                  pl.BlockSpec((1, n_mem, D_MODEL), lambda i, t: (i, 0, 0)),
                  pl.BlockSpec((1, n_mem, D_MODEL), lambda i, t: (i, 0, 1)),
                  _layer_spec((D_MODEL, D_MODEL), layer)],
        out_specs=seq,
        out_shape=jax.ShapeDtypeStruct((b, s, D_MODEL), F32),
        scratch_shapes=[pltpu.VMEM((ts, D_MODEL), BF16)],
        compiler_params=_params("parallel", "parallel"),
        name="xattn",
    )(x, g_in, g_out, w_q, kv, kv, w_o)


def _rope_tables(s):
    half = RET_HEAD_DIM // 2
    inv_freq = ROPE_BASE ** (-jnp.arange(half, dtype=F32) / half)
    ang = jnp.arange(s, dtype=F32)[:, None] * inv_freq[None, :]
    cos, sin = jnp.cos(ang), jnp.sin(ang)
    return (jnp.tile(cos, (1, LANES // half)),
            jnp.tile(jnp.concatenate([-sin, sin], axis=1), (1, LANES // RET_HEAD_DIM)))


def _decay_rows(decay_fwd, decay_bwd):
    pair = lambda d: jnp.repeat(d, RET_HEAD_DIM).reshape(N_PAIRS, LANES)
    head = lambda d: jnp.broadcast_to(d[:, None], (RET_HEADS, LANES))
    return jnp.concatenate([pair(decay_fwd), pair(decay_bwd), head(decay_fwd), head(decay_bwd)], axis=0)


def _trunk(x, mem, w, small, cos, sin, pavg):
    b, s, _ = x.shape
    t = b * s
    n_mem = mem.shape[1]
    row = lambda a: a.reshape(1, -1)
    rows8 = lambda a: jnp.broadcast_to(a[..., None, :], a.shape[:-1] + (SUBLANES, a.shape[-1]))
    flat = lambda a: a.reshape(t, a.shape[-1])
    seq = lambda a: a.reshape(b, s, a.shape[-1])
    mem2 = mem.reshape(b * n_mem, D_MODEL)
    x = flat(x)
    for l, p in enumerate(small):
        ng = p["norm_g"]
        x = _ffn(x, row(ng[0]), row(ng[1]), w["ffn1_w_gu"], w["ffn1_w_down"], l)
        proj = _mixer_in(p["dec"], x, row(ng[2]), w["w_in"], row(p["gate_b"]), cos, sin, s, l)
        u, q, qf, qb, kf, kb, k, v, sg, gates = map(seq, proj)
        sf, sb = _ret_state(p["dec"], kf, kb, v)
        x = _mixer(p["dec"], seq(x), u, q, qf, qb, k, v, sg, gates, sf, sb, rows8(p["conv_dw_w"]),
                   rows8(p["conv_dw_b"]), rows8(p["conv_ln_g"]), rows8(p["conv_ln_b"]), w["conv_w_pw"],
                   row(p["ret_gn_g"]), w["ret_w_out"], w["w_mix_out"], row(ng[3]), pavg, l)
        kv = _mem_kv(mem2, row(p["mem_norm_g"]), w["xattn_w_kv"], l).reshape(b, n_mem, 2 * D_MODEL)
        x = _xattn(x, kv, row(ng[4]), row(ng[5]), w["xattn_w_q"], w["xattn_w_o"], l)
        x = _ffn(flat(x), row(ng[6]), row(ng[7]), w["ffn2_w_gu"], w["ffn2_w_down"], l)
    return seq(x)


def kernel(x_prompt, x_sample, mem_prompt, mem_sample, norm_g, ffn1_w_gu, ffn1_w_down, w_in, conv_dw_w, conv_dw_b,
           conv_ln_g, conv_ln_b, conv_w_pw, ret_decay_fwd, ret_decay_bwd, ret_gn_g, ret_w_out, gate_b, w_mix_out,
           mem_norm_g, xattn_w_q, xattn_w_kv, xattn_w_o, ffn2_w_gu, ffn2_w_down):
    w = {name: a.astype(BF16) for name, a in dict(
        ffn1_w_gu=ffn1_w_gu, ffn1_w_down=ffn1_w_down, w_in=w_in, conv_w_pw=conv_w_pw, ret_w_out=ret_w_out,
        w_mix_out=w_mix_out, xattn_w_q=xattn_w_q, xattn_w_kv=xattn_w_kv, xattn_w_o=xattn_w_o,
        ffn2_w_gu=ffn2_w_gu, ffn2_w_down=ffn2_w_down).items()}
    small = [dict(norm_g=norm_g[l], conv_dw_w=conv_dw_w[l], conv_dw_b=conv_dw_b[l], conv_ln_g=conv_ln_g[l],
                  conv_ln_b=conv_ln_b[l], dec=_decay_rows(ret_decay_fwd[l], ret_decay_bwd[l]), ret_gn_g=ret_gn_g[l],
                  gate_b=gate_b[l], mem_norm_g=mem_norm_g[l]) for l in range(norm_g.shape[0])]
    cos, sin = _rope_tables(max(x_prompt.shape[1], x_sample.shape[1]))
    head_of = jnp.arange(RET_DIM) // RET_HEAD_DIM
    pavg = jnp.where(head_of[:, None] == head_of[None, :], 1.0 / RET_HEAD_DIM, 0.0).astype(BF16)
    return (_trunk(x_prompt, mem_prompt, w, small, cos, sin, pavg),
            _trunk(x_sample, mem_sample, w, small, cos, sin, pavg))
```
